```python
import jax, jax.numpy as jnp
from jax import lax
import numpy as np

D_MODEL = 2048
BATCH = 8
SEQ = 2048
DEPTH = 2

POOL_WINDOWS = (2, 4, 8, 16)
POOL_GROUPS = 4
POOL_GROUP_DIM = D_MODEL // 8
POOL_WIDTH = POOL_GROUPS * POOL_GROUP_DIM
MAX_POOL_WINDOW = 16
HEAD_DIM = 64
N_Q_HEADS = D_MODEL // 128
N_KV_HEADS = 4
Q_PER_KV = N_Q_HEADS // N_KV_HEADS
ATTN_WIDTH = N_Q_HEADS * HEAD_DIM
KV_WIDTH = N_KV_HEADS * HEAD_DIM
WINDOW = 128
BLOCK = 128
N_BRANCHES = 2
DEEPNORM_ALPHA = (2 * DEPTH) ** 0.25
DEEPNORM_BETA = (8 * DEPTH) ** -0.25
LN_EPS = 1e-5
OFF_PGATE = POOL_WIDTH
OFF_Q = 2 * POOL_WIDTH
OFF_K = OFF_Q + ATTN_WIDTH
OFF_V = OFF_K + KV_WIDTH
OFF_AGATE = OFF_V + KV_WIDTH
OFF_MERGE = OFF_AGATE + ATTN_WIDTH
N_IN = OFF_MERGE + N_BRANCHES * D_MODEL

kernel_name = "hybrid_pool_swa_sink_gated_deepnorm"


def _layernorm(x, g, b):
    xf = x.astype(jnp.float32)
    mu = jnp.mean(xf, axis=-1, keepdims=True)
    var = jnp.mean(jnp.square(xf - mu), axis=-1, keepdims=True)
    return ((xf - mu) * lax.rsqrt(var + LN_EPS) * g + b).astype(x.dtype)


def _pool_mixer(p, w_grp, scale):
    B, S, _ = p.shape
    cs = jnp.cumsum(p.astype(jnp.float32), axis=1)
    cs = jnp.pad(cs, ((0, 0), (MAX_POOL_WINDOW, 0), (0, 0)))
    pos = jnp.arange(S)
    pooled = []
    for g, w in enumerate(POOL_WINDOWS):
        lo, hi = g * POOL_GROUP_DIM, (g + 1) * POOL_GROUP_DIM
        win_sum = cs[:, MAX_POOL_WINDOW:, lo:hi] - cs[:, MAX_POOL_WINDOW - w:MAX_POOL_WINDOW - w + S, lo:hi]
        count = jnp.minimum(pos + 1, w).astype(jnp.float32)[None, :, None]
        pooled.append(win_sum / count)
    pooled = jnp.concatenate(pooled, axis=-1).astype(p.dtype) - p
    pooled = pooled.reshape(B, S, POOL_GROUPS, POOL_GROUP_DIM)
    mixed = jnp.einsum('bsgc,gcd->bsgd', pooled, w_grp).reshape(B, S, POOL_WIDTH)
    return mixed * scale


def _sliding_window_attention(q, k, v, sinks):
    B, S, _ = q.shape
    nb = S // BLOCK
    qb = q.reshape(B, nb, BLOCK, N_KV_HEADS, Q_PER_KV, HEAD_DIM)
    kb = k.reshape(B, nb, BLOCK, N_KV_HEADS, HEAD_DIM)
    vb = v.reshape(B, nb, BLOCK, N_KV_HEADS, HEAD_DIM)

    def with_prev(t):
        prev = jnp.pad(t[:, :-1], ((0, 0), (1, 0), (0, 0), (0, 0), (0, 0)))
        return jnp.concatenate([prev, t], axis=2)

    kw, vw = with_prev(kb), with_prev(vb)
    scores = jnp.einsum('bnqhgd,bnkhd->bnhgqk', qb, kw).astype(jnp.float32) * (HEAD_DIM ** -0.5)
    qi = jnp.arange(BLOCK)[:, None]
    kj = jnp.arange(2 * BLOCK)[None, :]
    rel = BLOCK + qi - kj
    band = (rel >= 0) & (rel < WINDOW)
    key_pos = jnp.arange(nb)[:, None] * BLOCK - BLOCK + kj
    valid = band[None] & (key_pos >= 0)[:, None, :]
    scores = jnp.where(valid[None, :, None, None], scores, -jnp.inf)
    sink = sinks.astype(jnp.float32).reshape(N_KV_HEADS, Q_PER_KV)[None, None, :, :, None, None]
    m = jnp.maximum(jnp.max(scores, axis=-1, keepdims=True), sink)
    e = jnp.exp(scores - m)
    probs = (e / (jnp.sum(e, axis=-1, keepdims=True) + jnp.exp(sink - m))).astype(v.dtype)
    o = jnp.einsum('bnhgqk,bnkhd->bnqhgd', probs, vw)
    return o.reshape(B, S, ATTN_WIDTH)


def _layer(x, c, w_ada, b_ada, w_in, w_pool_grp, pool_scale, sinks, w_pool_up, w_attn_up, w_out, ln_g, ln_b):
    mod = jax.nn.silu(c) @ w_ada + b_ada
    shift, scale, gate = jnp.split(mod, 3, axis=-1)
    u = x * (1 + scale[:, None, :]) + shift[:, None, :]
    h = u @ w_in
    p_in, p_gate, q, k, v, a_gate, g_merge = jnp.split(
        h, [OFF_PGATE, OFF_Q, OFF_K, OFF_V, OFF_AGATE, OFF_MERGE], axis=-1)
    y_pool = _pool_mixer(p_in, w_pool_grp, pool_scale) * jax.nn.silu(p_gate)
    y_attn = _sliding_window_attention(q, k, v, sinks) * jax.nn.silu(a_gate)
    g_pool, g_attn = jnp.split(jax.nn.sigmoid(g_merge), N_BRANCHES, axis=-1)
    merged = g_pool * (y_pool @ w_pool_up) + g_attn * (y_attn @ w_attn_up)
    out = merged @ w_out
    return _layernorm(DEEPNORM_ALPHA * x + gate[:, None, :] * out, ln_g, ln_b)


def setup_inputs(seed: int = 0) -> dict:
    key = jax.random.key(seed)
    ks = jax.random.split(key, 14)
    D = D_MODEL
    f32 = jnp.float32
    nrm = lambda k, shape, s: jax.random.normal(k, shape, f32) * s
    col_scale = jnp.ones((N_IN,), f32).at[OFF_V:OFF_AGATE].set(DEEPNORM_BETA)
    return {
        "x": nrm(ks[0], (BATCH, SEQ, D), 1.0),
        "c": nrm(ks[1], (BATCH, D), 1.0),
        "w_ada": nrm(ks[2], (DEPTH, D, 3 * D), D ** -0.5),
        "b_ada": nrm(ks[3], (DEPTH, 3 * D), 0.02),
        "w_in": nrm(ks[4], (DEPTH, D, N_IN), D ** -0.5) * col_scale,
        "w_pool_grp": nrm(ks[5], (DEPTH, POOL_GROUPS, POOL_GROUP_DIM, POOL_GROUP_DIM), POOL_GROUP_DIM ** -0.5),
        "pool_scale": 1.0 + nrm(ks[6], (DEPTH, POOL_WIDTH), 0.1),
        "sinks": nrm(ks[7], (DEPTH, N_Q_HEADS), 1.0),
        "w_pool_up": nrm(ks[8], (DEPTH, POOL_WIDTH, D), POOL_WIDTH ** -0.5 * DEEPNORM_BETA),
        "w_attn_up": nrm(ks[9], (DEPTH, ATTN_WIDTH, D), ATTN_WIDTH ** -0.5 * DEEPNORM_BETA),
        "w_out": nrm(ks[10], (DEPTH, D, D), D ** -0.5 * DEEPNORM_BETA),
        "ln_g": 1.0 + nrm(ks[11], (DEPTH, D), 0.05),
        "ln_b": nrm(ks[12], (DEPTH, D), 0.02),
    }


def reference(x, c, w_ada, b_ada, w_in, w_pool_grp, pool_scale, sinks, w_pool_up, w_attn_up, w_out, ln_g, ln_b):
    for l in range(DEPTH):
        x = _layer(x, c, w_ada[l], b_ada[l], w_in[l], w_pool_grp[l], pool_scale[l], sinks[l],
                   w_pool_up[l], w_attn_up[l], w_out[l], ln_g[l], ln_b[l])
    return x
```

```python
import functools

import jax
import jax.numpy as jnp
from jax import lax
from jax.experimental import pallas as pl
from jax.experimental.pallas import tpu as pltpu

F32 = jnp.float32
BF16 = jnp.bfloat16

POOL_WINDOWS = (2, 4, 8, 16)
POOL_GROUP_DIM = 256
MAX_POOL_WINDOW = 16
HEAD_DIM = 64
N_KV_HEADS = 4
Q_PER_KV = 4
ATTN_BLOCK = 128
LN_EPS = 1e-5

V7X_VMEM_BYTES = 64 * 1024 * 1024
V7X_VMEM_LIMIT_CAP = 60000 * 1024
BF16_SUBLANES = 16

ADA_COLS = 1536
INPROJ_ROWS = 256
INPROJ_COLS = 512
MIX_ROWS = 256


def _vmem_limit(estimate_bytes):
    return int(min(V7X_VMEM_LIMIT_CAP, max(32 * 1024 * 1024, estimate_bytes)))


def _ada_kernel(c_ref, w_ref, b_ref, o_ref):
    sc = jax.nn.silu(c_ref[...]).astype(BF16)
    acc = jnp.dot(sc, w_ref[...].astype(BF16), preferred_element_type=F32)
    o_ref[...] = acc + b_ref[...]


def _ada_call(c, w_ada, b_ada):
    depth, d, n = w_ada.shape
    b = c.shape[0]
    return pl.pallas_call(
        _ada_kernel,
        grid=(depth, n // ADA_COLS),
        in_specs=[
            pl.BlockSpec((b, d), lambda l, j: (0, 0)),
            pl.BlockSpec((None, d, ADA_COLS), lambda l, j: (l, 0, j)),
            pl.BlockSpec((None, 1, ADA_COLS), lambda l, j: (l, 0, j)),
        ],
        out_specs=pl.BlockSpec((None, b, ADA_COLS), lambda l, j: (l, 0, j)),
        out_shape=jax.ShapeDtypeStruct((depth, b, n), F32),
        compiler_params=pltpu.CompilerParams(
            dimension_semantics=("arbitrary", "arbitrary"),
            vmem_limit_bytes=_vmem_limit(2 * d * ADA_COLS * 4 + 8 * 1024 * 1024),
        ),
        name="ada_mod",
    )(c, w_ada, b_ada.reshape(depth, 1, n))


def _inproj_body(u_ref, w_ref, outs, widths):
    pin_ref, pgate_ref, q_ref, k_ref, v_ref, agate_ref, g_ref = outs
    epilogues = (
        (pin_ref, None),
        (pgate_ref, jax.nn.silu),
        (q_ref, lambda a: a * (HEAD_DIM ** -0.5)),
        (k_ref, None),
        (v_ref, None),
        (agate_ref, jax.nn.silu),
        (g_ref, jax.nn.sigmoid),
    )
    col = 0
    for (o_ref, fn), width in zip(epilogues, widths):
        step = min(INPROJ_COLS, width)
        for c in range(0, width, step):
            acc = jnp.dot(u_ref[...], w_ref[:, col + c:col + c + step],
                          preferred_element_type=F32)
            if fn is not None:
                acc = fn(acc)
            o_ref[:, c:c + step] = acc.astype(o_ref.dtype)
        col += width


def _inproj_kernel_x(x_ref, scale_ref, shift_ref, w_ref, *rest, widths):
    outs, u_scr = rest[:-1], rest[-1]
    u_scr[...] = (x_ref[...] * (1.0 + scale_ref[...]) + shift_ref[...]).astype(BF16)
    _inproj_body(u_scr, w_ref, outs, widths)


def _inproj_kernel_u(u_ref, w_ref, *outs, widths):
    _inproj_body(u_ref, w_ref, outs, widths)


def _inproj_call(x_or_u, scale, shift, w_in, widths, seq):
    m, d = x_or_u.shape
    n_in = w_in.shape[1]
    tiles_per_seq = seq // INPROJ_ROWS
    row_spec = lambda width: pl.BlockSpec((INPROJ_ROWS, width), lambda i: (i, 0))
    w_spec = pl.BlockSpec((d, n_in), lambda i: (0, 0), pipeline_mode=pl.Buffered(1))
    out_shape = [jax.ShapeDtypeStruct((m, width), BF16) for width in widths]
    out_specs = [row_spec(width) for width in widths]
    vmem = (d * n_in * 2 + 2 * INPROJ_ROWS * d * 4 + 2 * INPROJ_ROWS * n_in * 2
            + INPROJ_ROWS * d * 2 + 6 * 1024 * 1024)
    params = pltpu.CompilerParams(dimension_semantics=("arbitrary",),
                                  vmem_limit_bytes=_vmem_limit(vmem))
    if scale is not None:
        mod_spec = pl.BlockSpec((None, 1, d), lambda i: (i // tiles_per_seq, 0, 0))
        return pl.pallas_call(
            functools.partial(_inproj_kernel_x, widths=widths),
            grid=(m // INPROJ_ROWS,),
            in_specs=[row_spec(d), mod_spec, mod_spec, w_spec],
            out_specs=out_specs,
            out_shape=out_shape,
            scratch_shapes=[pltpu.VMEM((INPROJ_ROWS, d), BF16)],
            compiler_params=params,
            name="inproj_x",
        )(x_or_u, scale, shift, w_in)
    return pl.pallas_call(
        functools.partial(_inproj_kernel_u, widths=widths),
        grid=(m // INPROJ_ROWS,),
        in_specs=[row_spec(d), w_spec],
        out_specs=out_specs,
        out_shape=out_shape,
        compiler_params=params,
        name="inproj_u",
    )(x_or_u, w_in)


def _pool_mixer(pin_ref, pprev_ref, wgrp_ref, first_tile, tile_pos):
    rows = pin_ref.shape[0]
    p = pin_ref[...].astype(F32)
    prev = jnp.where(first_tile, 0.0, pprev_ref[...].astype(F32))
    ext = jnp.concatenate([prev, p], axis=0)
    pos = tile_pos + lax.broadcasted_iota(jnp.int32, (rows, 1), 0)
    mixed = []
    for g, w in enumerate(POOL_WINDOWS):
        lo, hi = g * POOL_GROUP_DIM, (g + 1) * POOL_GROUP_DIM
        s = ext[:, lo:hi]
        span = 1
        while span < w:
            s = s + pltpu.roll(s, span, 0)
            span *= 2
        win_sum = s[MAX_POOL_WINDOW:, :]
        count = jnp.minimum(pos + 1, w).astype(F32)
        pooled = win_sum / count - p[:, lo:hi]
        mixed.append(jnp.dot(pooled.astype(BF16), wgrp_ref[g], preferred_element_type=F32))
    return jnp.concatenate(mixed, axis=1)


def _attention(q_ref, k_ref, kprev_ref, v_ref, vprev_ref, sinks_ref, first_tile):
    rows = q_ref.shape[0]
    stack = Q_PER_KV * ATTN_BLOCK
    kext = jnp.concatenate([kprev_ref[...], k_ref[...]], axis=0)
    vext = jnp.concatenate([vprev_ref[...], v_ref[...]], axis=0)
    qi = lax.broadcasted_iota(jnp.int32, (stack, 2 * ATTN_BLOCK), 0) & (ATTN_BLOCK - 1)
    kj = lax.broadcasted_iota(jnp.int32, (stack, 2 * ATTN_BLOCK), 1)
    band = (kj > qi) & (kj <= qi + ATTN_BLOCK)
    bias_inner = jnp.where(band, 0.0, -jnp.inf).astype(F32)
    bias_first = jnp.where(band & ((kj >= ATTN_BLOCK) | jnp.logical_not(first_tile)),
                           0.0, -jnp.inf).astype(F32)
    blocks = []
    for qb in range(rows // ATTN_BLOCK):
        r0 = qb * ATTN_BLOCK
        bias = bias_first if qb == 0 else bias_inner
        pieces = []
        for h in range(N_KV_HEADS):
            qh = q_ref[r0:r0 + ATTN_BLOCK, h * Q_PER_KV * HEAD_DIM:(h + 1) * Q_PER_KV * HEAD_DIM]
            qs = jnp.concatenate(
                [qh[:, g * HEAD_DIM:(g + 1) * HEAD_DIM] for g in range(Q_PER_KV)], axis=0)
            kh = kext[r0:r0 + 2 * ATTN_BLOCK, h * HEAD_DIM:(h + 1) * HEAD_DIM]
            vh = vext[r0:r0 + 2 * ATTN_BLOCK, h * HEAD_DIM:(h + 1) * HEAD_DIM]
            s = lax.dot_general(qs, kh, (((1,), (1,)), ((), ())),
                                preferred_element_type=F32) + bias
            sink = jnp.concatenate(
                [jnp.full((ATTN_BLOCK, 1), sinks_ref[h * Q_PER_KV + g], F32)
                 for g in range(Q_PER_KV)], axis=0)
            mx = jnp.maximum(jnp.max(s, axis=1, keepdims=True), sink)
            e = jnp.exp(s - mx)
            denom = jnp.sum(e, axis=1, keepdims=True) + jnp.exp(sink - mx)
            o = jnp.dot(e.astype(BF16), vh, preferred_element_type=F32) / denom
            pieces.extend(o[g * ATTN_BLOCK:(g + 1) * ATTN_BLOCK] for g in range(Q_PER_KV))
        blocks.append(jnp.concatenate(pieces, axis=1))
    return jnp.concatenate(blocks, axis=0)


def _mix_kernel(sinks_ref, pin_ref, pprev_ref, pgate_ref, q_ref, k_ref, kprev_ref, v_ref,
                vprev_ref, agate_ref, g_ref, x_ref, gate_ref, wgrp_ref, pscale_ref,
                wpu_ref, wau_ref, wout_ref, lng_ref, lnb_ref, *rest, alpha, emit_next):
    if emit_next:
        nscale_ref, nshift_ref, xo_ref, uo_ref = rest
    else:
        (xo_ref,) = rest
    rows, d = x_ref.shape
    t = pl.program_id(1)
    first_tile = t == 0

    mixed = _pool_mixer(pin_ref, pprev_ref, wgrp_ref, first_tile, t * rows)
    y_pool = (mixed * pscale_ref[...] * pgate_ref[...].astype(F32)).astype(BF16)
    attn = _attention(q_ref, k_ref, kprev_ref, v_ref, vprev_ref, sinks_ref, first_tile)
    y_attn = (attn * agate_ref[...].astype(F32)).astype(BF16)

    up_pool = jnp.dot(y_pool, wpu_ref[...], preferred_element_type=F32)
    up_attn = jnp.dot(y_attn, wau_ref[...], preferred_element_type=F32)
    merged = (g_ref[:, :d].astype(F32) * up_pool + g_ref[:, d:].astype(F32) * up_attn)
    out = jnp.dot(merged.astype(BF16), wout_ref[...], preferred_element_type=F32)

    y = alpha * x_ref[...] + gate_ref[...] * out
    mu = jnp.mean(y, axis=-1, keepdims=True)
    yc = y - mu
    var = jnp.mean(yc * yc, axis=-1, keepdims=True)
    xn = yc * lax.rsqrt(var + LN_EPS) * lng_ref[...] + lnb_ref[...]
    xo_ref[...] = xn
    if emit_next:
        uo_ref[...] = (xn * (1.0 + nscale_ref[...]) + nshift_ref[...]).astype(BF16)


def _mix_call(h_parts, x, gate, next_mod, wgrp, pscale, sinks, wpu, wau, wout, lng, lnb,
              batch, seq, alpha):
    pin, pgate, q, k, v, agate, g = h_parts
    m, d = x.shape
    nt = seq // MIX_ROWS
    emit_next = next_mod is not None

    row = lambda width: pl.BlockSpec((MIX_ROWS, width), lambda b, t: (b * nt + t, 0))

    def prev_spec(block_rows, width):
        per_tile, per_seq = MIX_ROWS // block_rows, seq // block_rows
        return pl.BlockSpec(
            (block_rows, width),
            lambda b, t: (b * per_seq + jnp.maximum(t * per_tile - 1, 0), 0))

    full = lambda a: pl.BlockSpec(a.shape, lambda b, t: (0,) * a.ndim,
                                  pipeline_mode=pl.Buffered(1))
    mod_spec = pl.BlockSpec((None, 1, d), lambda b, t: (b, 0, 0))

    in_specs = [
        pl.BlockSpec(memory_space=pltpu.SMEM),
        row(pin.shape[1]), prev_spec(MAX_POOL_WINDOW, pin.shape[1]), row(pgate.shape[1]),
        row(q.shape[1]),
        row(k.shape[1]), prev_spec(ATTN_BLOCK, k.shape[1]),
        row(v.shape[1]), prev_spec(ATTN_BLOCK, v.shape[1]),
        row(agate.shape[1]), row(g.shape[1]), row(d), mod_spec,
        full(wgrp), full(pscale), full(wpu), full(wau), full(wout), full(lng), full(lnb),
    ]
    args = [sinks, pin, pin, pgate, q, k, k, v, v, agate, g, x, gate,
            wgrp, pscale, wpu, wau, wout, lng, lnb]
    out_shape = [jax.ShapeDtypeStruct((m, d), F32)]
    out_specs = [row(d)]
    if emit_next:
        in_specs += [mod_spec, mod_spec]
        args += list(next_mod)
        out_shape.append(jax.ShapeDtypeStruct((m, d), BF16))
        out_specs.append(row(d))

    h_cols = sum(a.shape[1] for a in h_parts)
    weights = 2 * (wgrp.size + wpu.size + wau.size + wout.size)
    tiles = 2 * MIX_ROWS * (h_cols * 2 + d * 4 + d * 4 + d * 2)
    temps = 10 * MIX_ROWS * d * 4
    return pl.pallas_call(
        functools.partial(_mix_kernel, alpha=alpha, emit_next=emit_next),
        grid=(batch, nt),
        in_specs=in_specs,
        out_specs=out_specs,
        out_shape=out_shape,
        compiler_params=pltpu.CompilerParams(
            dimension_semantics=("arbitrary", "arbitrary"),
            vmem_limit_bytes=_vmem_limit(weights + tiles + temps),
        ),
        name="mixer",
    )(*args)


def kernel(x, c, w_ada, b_ada, w_in, w_pool_grp, pool_scale, sinks, w_pool_up, w_attn_up,
           w_out, ln_g, ln_b):
    batch, seq, d = x.shape
    depth = w_ada.shape[0]
    pool_width = w_pool_grp.shape[1] * w_pool_grp.shape[2]
    attn_width = w_attn_up.shape[1]
    kv_width = N_KV_HEADS * HEAD_DIM
    widths = (pool_width, pool_width, attn_width, kv_width, kv_width, attn_width, 2 * d)
    assert sum(widths) == w_in.shape[2]
    assert seq % MIX_ROWS == 0 and seq % INPROJ_ROWS == 0 and MIX_ROWS % ATTN_BLOCK == 0
    alpha = (2 * depth) ** 0.25

    mod = _ada_call(c, w_ada, b_ada)
    shift, scale, gate = [mod[:, :, i * d:(i + 1) * d].reshape(depth, batch, 1, d)
                          for i in range(3)]

    xf = x.reshape(batch * seq, d)
    u = None
    for l in range(depth):
        w_in_l = w_in[l].astype(BF16)
        if l == 0:
            h_parts = _inproj_call(xf, scale[l], shift[l], w_in_l, widths, seq)
        else:
            h_parts = _inproj_call(u, None, None, w_in_l, widths, seq)
        next_mod = (scale[l + 1], shift[l + 1]) if l + 1 < depth else None
        outs = _mix_call(
            h_parts, xf, gate[l], next_mod,
            w_pool_grp[l].astype(BF16), pool_scale[l].reshape(1, -1), sinks[l],
            w_pool_up[l].astype(BF16), w_attn_up[l].astype(BF16), w_out[l].astype(BF16),
            ln_g[l].reshape(1, d), ln_b[l].reshape(1, d), batch, seq, alpha)
        xf = outs[0]
        if next_mod is not None:
            u = outs[1]
    return xf.reshape(batch, seq, d)
```

```python
import functools

import jax
import jax.numpy as jnp
from jax import lax
from jax.experimental import pallas as pl
from jax.experimental.pallas import tpu as pltpu

F32 = jnp.float32
BF16 = jnp.bfloat16

POOL_WINDOWS = (2, 4, 8, 16)
POOL_GROUP_DIM = 256
MAX_POOL_WINDOW = 16
HEAD_DIM = 64
N_KV_HEADS = 4
Q_PER_KV = 4
ATTN_BLOCK = 128
LN_EPS = 1e-5

V7X_VMEM_BYTES = 64 * 1024 * 1024
V7X_VMEM_LIMIT_CAP = 60000 * 1024
BF16_SUBLANES = 16

ADA_COLS = 1536
INPROJ_ROWS = 256
INPROJ_COLS = 512
MIX_ROWS = 256


def _vmem_limit(estimate_bytes):
    return int(min(V7X_VMEM_LIMIT_CAP, max(32 * 1024 * 1024, estimate_bytes)))


def _ada_kernel(c_ref, w_ref, b_ref, o_ref):
    sc = jax.nn.silu(c_ref[...]).astype(BF16)
    acc = jnp.dot(sc, w_ref[...].astype(BF16), preferred_element_type=F32)
    o_ref[...] = acc + b_ref[...]


def _ada_call(c, w_ada, b_ada):
    depth, d, n = w_ada.shape
    b = c.shape[0]
    return pl.pallas_call(
        _ada_kernel,
        grid=(depth, n // ADA_COLS),
        in_specs=[
            pl.BlockSpec((b, d), lambda l, j: (0, 0)),
            pl.BlockSpec((None, d, ADA_COLS), lambda l, j: (l, 0, j)),
            pl.BlockSpec((None, 1, ADA_COLS), lambda l, j: (l, 0, j)),
        ],
        out_specs=pl.BlockSpec((None, b, ADA_COLS), lambda l, j: (l, 0, j)),
        out_shape=jax.ShapeDtypeStruct((depth, b, n), F32),
        compiler_params=pltpu.CompilerParams(
            dimension_semantics=("arbitrary", "arbitrary"),
            vmem_limit_bytes=_vmem_limit(2 * d * ADA_COLS * 4 + 8 * 1024 * 1024),
        ),
        name="ada_mod",
    )(c, w_ada, b_ada.reshape(depth, 1, n))


def _inproj_body(u_ref, w_ref, outs, widths):
    pin_ref, pgate_ref, q_ref, k_ref, v_ref, agate_ref, g_ref = outs
    epilogues = (
        (pin_ref, None),
        (pgate_ref, jax.nn.silu),
        (q_ref, lambda a: a * (HEAD_DIM ** -0.5)),
        (k_ref, None),
        (v_ref, None),
        (agate_ref, jax.nn.silu),
        (g_ref, jax.nn.sigmoid),
    )
    col = 0
    for (o_ref, fn), width in zip(epilogues, widths):
        step = min(INPROJ_COLS, width)
        for c in range(0, width, step):
            acc = jnp.dot(u_ref[...], w_ref[:, col + c:col + c + step],
                          preferred_element_type=F32)
            if fn is not None:
                acc = fn(acc)
            o_ref[:, c:c + step] = acc.astype(o_ref.dtype)
        col += width


def _inproj_kernel_x(x_ref, scale_ref, shift_ref, w_ref, *rest, widths):
    outs, u_scr = rest[:-1], rest[-1]
    u_scr[...] = (x_ref[...] * (1.0 + scale_ref[...]) + shift_ref[...]).astype(BF16)
    _inproj_body(u_scr, w_ref, outs, widths)


def _inproj_kernel_u(u_ref, w_ref, *outs, widths):
    _inproj_body(u_ref, w_ref, outs, widths)


def _inproj_call(x_or_u, scale, shift, w_in, widths, seq):
    m, d = x_or_u.shape
    n_in = w_in.shape[1]
    tiles_per_seq = seq // INPROJ_ROWS
    row_spec = lambda width: pl.BlockSpec((INPROJ_ROWS, width), lambda i: (i, 0))
    w_spec = pl.BlockSpec((d, n_in), lambda i: (0, 0), pipeline_mode=pl.Buffered(1))
    out_shape = [jax.ShapeDtypeStruct((m, width), BF16) for width in widths]
    out_specs = [row_spec(width) for width in widths]
    vmem = (d * n_in * 2 + 2 * INPROJ_ROWS * d * 4 + 2 * INPROJ_ROWS * n_in * 2
            + INPROJ_ROWS * d * 2 + 6 * 1024 * 1024)
    params = pltpu.CompilerParams(dimension_semantics=("arbitrary",),
                                  vmem_limit_bytes=_vmem_limit(vmem))
    if scale is not None:
        mod_spec = pl.BlockSpec((None, 1, d), lambda i: (i // tiles_per_seq, 0, 0))
        return pl.pallas_call(
            functools.partial(_inproj_kernel_x, widths=widths),
            grid=(m // INPROJ_ROWS,),
            in_specs=[row_spec(d), mod_spec, mod_spec, w_spec],
            out_specs=out_specs,
            out_shape=out_shape,
            scratch_shapes=[pltpu.VMEM((INPROJ_ROWS, d), BF16)],
            compiler_params=params,
            name="inproj_x",
        )(x_or_u, scale, shift, w_in)
    return pl.pallas_call(
        functools.partial(_inproj_kernel_u, widths=widths),
        grid=(m // INPROJ_ROWS,),
        in_specs=[row_spec(d), w_spec],
        out_specs=out_specs,
        out_shape=out_shape,
        compiler_params=params,
        name="inproj_u",
    )(x_or_u, w_in)


def _pool_mixer(pin_ref, pprev_ref, wgrp_ref, first_tile, tile_pos):
    rows = pin_ref.shape[0]
    p = pin_ref[...].astype(F32)
    prev = jnp.where(first_tile, 0.0, pprev_ref[...].astype(F32))
    ext = jnp.concatenate([prev, p], axis=0)
    pos = tile_pos + lax.broadcasted_iota(jnp.int32, (rows, 1), 0)
    mixed = []
    for g, w in enumerate(POOL_WINDOWS):
        lo, hi = g * POOL_GROUP_DIM, (g + 1) * POOL_GROUP_DIM
        s = ext[:, lo:hi]
        span = 1
        while span < w:
            s = s + pltpu.roll(s, span, 0)
            span *= 2
        win_sum = s[MAX_POOL_WINDOW:, :]
        count = jnp.minimum(pos + 1, w).astype(F32)
        pooled = win_sum / count - p[:, lo:hi]
        mixed.append(jnp.dot(pooled.astype(BF16), wgrp_ref[g], preferred_element_type=F32))
    return jnp.concatenate(mixed, axis=1)


def _attention(q_ref, k_ref, kprev_ref, v_ref, vprev_ref, sinks_ref, first_tile):
    rows = q_ref.shape[0]
    stack = Q_PER_KV * ATTN_BLOCK
    kext = jnp.concatenate([kprev_ref[...], k_ref[...]], axis=0)
    vext = jnp.concatenate([vprev_ref[...], v_ref[...]], axis=0)
    kj = lax.broadcasted_iota(jnp.int32, (2 * ATTN_BLOCK, stack), 0)
    qi = lax.broadcasted_iota(jnp.int32, (2 * ATTN_BLOCK, stack), 1) & (ATTN_BLOCK - 1)
    band = (kj > qi) & (kj <= qi + ATTN_BLOCK)
    bias_inner = jnp.where(band, 0.0, -jnp.inf).astype(F32)
    bias_first = jnp.where(band & ((kj >= ATTN_BLOCK) | jnp.logical_not(first_tile)),
                           0.0, -jnp.inf).astype(F32)
    blocks = []
    for qb in range(rows // ATTN_BLOCK):
        r0 = qb * ATTN_BLOCK
        bias = bias_first if qb == 0 else bias_inner
        pieces = []
        for h in range(N_KV_HEADS):
            qh = q_ref[r0:r0 + ATTN_BLOCK, h * Q_PER_KV * HEAD_DIM:(h + 1) * Q_PER_KV * HEAD_DIM]
            qs = jnp.concatenate(
                [qh[:, g * HEAD_DIM:(g + 1) * HEAD_DIM] for g in range(Q_PER_KV)], axis=0)
            kh = kext[r0:r0 + 2 * ATTN_BLOCK, h * HEAD_DIM:(h + 1) * HEAD_DIM]
            vh = vext[r0:r0 + 2 * ATTN_BLOCK, h * HEAD_DIM:(h + 1) * HEAD_DIM]
            st = lax.dot_general(kh, qs, (((1,), (1,)), ((), ())),
                                 preferred_element_type=F32) + bias
            sink = jnp.concatenate(
                [jnp.full((1, ATTN_BLOCK), sinks_ref[h * Q_PER_KV + g], F32)
                 for g in range(Q_PER_KV)], axis=1)
            mx = jnp.maximum(jnp.max(st, axis=0, keepdims=True), sink)
            e = jnp.exp(st - mx)
            denom = jnp.sum(e, axis=0, keepdims=True) + jnp.exp(sink - mx)
            probs = (e * (1.0 / denom)).astype(BF16)
            o = lax.dot_general(probs, vh, (((0,), (0,)), ((), ())),
                                preferred_element_type=F32)
            pieces.extend(o[g * ATTN_BLOCK:(g + 1) * ATTN_BLOCK] for g in range(Q_PER_KV))
        blocks.append(jnp.concatenate(pieces, axis=1))
    return jnp.concatenate(blocks, axis=0)


def _mix_kernel(sinks_ref, pin_ref, pprev_ref, pgate_ref, q_ref, k_ref, kprev_ref, v_ref,
                vprev_ref, agate_ref, g_ref, x_ref, gate_ref, wgrp_ref, pscale_ref,
                wpu_ref, wau_ref, wout_ref, lng_ref, lnb_ref, *rest, alpha, emit_next):
    if emit_next:
        nscale_ref, nshift_ref, xo_ref, uo_ref = rest
    else:
        (xo_ref,) = rest
    rows, d = x_ref.shape
    t = pl.program_id(1)
    first_tile = t == 0

    mixed = _pool_mixer(pin_ref, pprev_ref, wgrp_ref, first_tile, t * rows)
    y_pool = (mixed * pscale_ref[...] * pgate_ref[...].astype(F32)).astype(BF16)
    attn = _attention(q_ref, k_ref, kprev_ref, v_ref, vprev_ref, sinks_ref, first_tile)
    y_attn = (attn * agate_ref[...].astype(F32)).astype(BF16)

    up_pool = jnp.dot(y_pool, wpu_ref[...], preferred_element_type=F32)
    up_attn = jnp.dot(y_attn, wau_ref[...], preferred_element_type=F32)
    merged = (g_ref[:, :d].astype(F32) * up_pool + g_ref[:, d:].astype(F32) * up_attn)
    out = jnp.dot(merged.astype(BF16), wout_ref[...], preferred_element_type=F32)

    y = alpha * x_ref[...] + gate_ref[...] * out
    mu = jnp.mean(y, axis=-1, keepdims=True)
    yc = y - mu
    var = jnp.mean(yc * yc, axis=-1, keepdims=True)
    xn = yc * lax.rsqrt(var + LN_EPS) * lng_ref[...] + lnb_ref[...]
    xo_ref[...] = xn
    if emit_next:
        uo_ref[...] = (xn * (1.0 + nscale_ref[...]) + nshift_ref[...]).astype(BF16)


def _mix_call(h_parts, x, gate, next_mod, wgrp, pscale, sinks, wpu, wau, wout, lng, lnb,
              batch, seq, alpha):
    pin, pgate, q, k, v, agate, g = h_parts
    m, d = x.shape
    nt = seq // MIX_ROWS
    emit_next = next_mod is not None

    row = lambda width: pl.BlockSpec((MIX_ROWS, width), lambda b, t: (b * nt + t, 0))

    def prev_spec(block_rows, width):
        per_tile, per_seq = MIX_ROWS // block_rows, seq // block_rows
        return pl.BlockSpec(
            (block_rows, width),
            lambda b, t: (b * per_seq + jnp.maximum(t * per_tile - 1, 0), 0))

    full = lambda a: pl.BlockSpec(a.shape, lambda b, t: (0,) * a.ndim,
                                  pipeline_mode=pl.Buffered(1))
    mod_spec = pl.BlockSpec((None, 1, d), lambda b, t: (b, 0, 0))

    in_specs = [
        pl.BlockSpec(memory_space=pltpu.SMEM),
        row(pin.shape[1]), prev_spec(MAX_POOL_WINDOW, pin.shape[1]), row(pgate.shape[1]),
        row(q.shape[1]),
        row(k.shape[1]), prev_spec(ATTN_BLOCK, k.shape[1]),
        row(v.shape[1]), prev_spec(ATTN_BLOCK, v.shape[1]),
        row(agate.shape[1]), row(g.shape[1]), row(d), mod_spec,
        full(wgrp), full(pscale), full(wpu), full(wau), full(wout), full(lng), full(lnb),
    ]
    args = [sinks, pin, pin, pgate, q, k, k, v, v, agate, g, x, gate,
            wgrp, pscale, wpu, wau, wout, lng, lnb]
    out_shape = [jax.ShapeDtypeStruct((m, d), F32)]
    out_specs = [row(d)]
    if emit_next:
        in_specs += [mod_spec, mod_spec]
        args += list(next_mod)
        out_shape.append(jax.ShapeDtypeStruct((m, d), BF16))
        out_specs.append(row(d))

    h_cols = sum(a.shape[1] for a in h_parts)
    weights = 2 * (wgrp.size + wpu.size + wau.size + wout.size)
    tiles = 2 * MIX_ROWS * (h_cols * 2 + d * 4 + d * 4 + d * 2)
    temps = 10 * MIX_ROWS * d * 4
    return pl.pallas_call(
        functools.partial(_mix_kernel, alpha=alpha, emit_next=emit_next),
        grid=(batch, nt),
        in_specs=in_specs,
        out_specs=out_specs,
        out_shape=out_shape,
        compiler_params=pltpu.CompilerParams(
            dimension_semantics=("arbitrary", "arbitrary"),
            vmem_limit_bytes=_vmem_limit(weights + tiles + temps),
        ),
        name="mixer",
    )(*args)


def kernel(x, c, w_ada, b_ada, w_in, w_pool_grp, pool_scale, sinks, w_pool_up, w_attn_up,
           w_out, ln_g, ln_b):
    batch, seq, d = x.shape
    depth = w_ada.shape[0]
    pool_width = w_pool_grp.shape[1] * w_pool_grp.shape[2]
    attn_width = w_attn_up.shape[1]
    kv_width = N_KV_HEADS * HEAD_DIM
    widths = (pool_width, pool_width, attn_width, kv_width, kv_width, attn_width, 2 * d)
    assert sum(widths) == w_in.shape[2]
    assert seq % MIX_ROWS == 0 and seq % INPROJ_ROWS == 0 and MIX_ROWS % ATTN_BLOCK == 0
    alpha = (2 * depth) ** 0.25

    mod = _ada_call(c, w_ada, b_ada)
    shift, scale, gate = [mod[:, :, i * d:(i + 1) * d].reshape(depth, batch, 1, d)
                          for i in range(3)]

    xf = x.reshape(batch * seq, d)
    u = None
    for l in range(depth):
        w_in_l = w_in[l].astype(BF16)
        if l == 0:
            h_parts = _inproj_call(xf, scale[l], shift[l], w_in_l, widths, seq)
        else:
            h_parts = _inproj_call(u, None, None, w_in_l, widths, seq)
        next_mod = (scale[l + 1], shift[l + 1]) if l + 1 < depth else None
        outs = _mix_call(
            h_parts, xf, gate[l], next_mod,
            w_pool_grp[l].astype(BF16), pool_scale[l].reshape(1, -1), sinks[l],
            w_pool_up[l].astype(BF16), w_attn_up[l].astype(BF16), w_out[l].astype(BF16),
            ln_g[l].reshape(1, d), ln_b[l].reshape(1, d), batch, seq, alpha)
        xf = outs[0]
        if next_mod is not None:
            u = outs[1]
    return xf.reshape(batch, seq, d)
```

```python
import functools

import jax
import jax.numpy as jnp
from jax import lax
from jax.experimental import pallas as pl
from jax.experimental.pallas import tpu as pltpu

F32 = jnp.float32
BF16 = jnp.bfloat16

POOL_WINDOWS = (2, 4, 8, 16)
POOL_GROUP_DIM = 256
MAX_POOL_WINDOW = 16
HEAD_DIM = 64
N_KV_HEADS = 4
Q_PER_KV = 4
ATTN_BLOCK = 128
LN_EPS = 1e-5
LOG2_E = 1.4426950408889634

V7X_VMEM_LIMIT_CAP = 60000 * 1024
BF16_SUBLANES = 16

ADA_COLS = 1536
INPROJ_ROWS = 256
INPROJ_COLS = 512
MIX_ROWS = 256


def _vmem_limit(estimate_bytes):
    return int(min(V7X_VMEM_LIMIT_CAP, max(32 * 1024 * 1024, estimate_bytes)))


def _ada_kernel(c_ref, w_ref, b_ref, o_ref):
    sc = jax.nn.silu(c_ref[...]).astype(BF16)
    acc = jnp.dot(sc, w_ref[...].astype(BF16), preferred_element_type=F32)
    o_ref[...] = acc + b_ref[...]


def _ada_call(c, w_ada, b_ada):
    depth, d, n = w_ada.shape
    b = c.shape[0]
    return pl.pallas_call(
        _ada_kernel,
        grid=(depth, n // ADA_COLS),
        in_specs=[
            pl.BlockSpec((b, d), lambda l, j: (0, 0)),
            pl.BlockSpec((None, d, ADA_COLS), lambda l, j: (l, 0, j)),
            pl.BlockSpec((None, 1, ADA_COLS), lambda l, j: (l, 0, j)),
        ],
        out_specs=pl.BlockSpec((None, b, ADA_COLS), lambda l, j: (l, 0, j)),
        out_shape=jax.ShapeDtypeStruct((depth, b, n), F32),
        compiler_params=pltpu.CompilerParams(
            dimension_semantics=("arbitrary", "arbitrary"),
            vmem_limit_bytes=_vmem_limit(2 * d * ADA_COLS * 4 + 8 * 1024 * 1024),
        ),
        name="ada_mod",
    )(c, w_ada, b_ada.reshape(depth, 1, n))


def _inproj_body(u_ref, w_ref, outs, widths):
    pin_ref, pgate_ref, q_ref, k_ref, v_ref, agate_ref, g_ref = outs
    epilogues = (
        (pin_ref, None),
        (pgate_ref, jax.nn.silu),
        (q_ref, lambda a: a * (HEAD_DIM ** -0.5 * LOG2_E)),
        (k_ref, None),
        (v_ref, None),
        (agate_ref, jax.nn.silu),
        (g_ref, jax.nn.sigmoid),
    )
    col = 0
    for (o_ref, fn), width in zip(epilogues, widths):
        step = min(INPROJ_COLS, width)
        for c in range(0, width, step):
            acc = jnp.dot(u_ref[...], w_ref[:, col + c:col + c + step],
                          preferred_element_type=F32)
            if fn is not None:
                acc = fn(acc)
            o_ref[:, c:c + step] = acc.astype(o_ref.dtype)
        col += width


def _inproj_kernel_x(x_ref, scale_ref, shift_ref, w_ref, *rest, widths, n_convert):
    slabs_in, rest = rest[:n_convert], rest[n_convert:]
    outs, slabs_out, u_scr = rest[:7], rest[7:7 + n_convert], rest[-1]
    u_scr[...] = (x_ref[...] * (1.0 + scale_ref[...]) + shift_ref[...]).astype(BF16)
    _inproj_body(u_scr, w_ref, outs, widths)
    for src, dst in zip(slabs_in, slabs_out):
        dst[...] = src[...].astype(BF16)


def _inproj_kernel_u(u_ref, w_ref, *outs, widths):
    _inproj_body(u_ref, w_ref, outs, widths)


def _inproj_call(x_or_u, scale, shift, w_in, widths, seq, convert=()):
    m, d = x_or_u.shape
    n_in = w_in.shape[1]
    steps = m // INPROJ_ROWS
    tiles_per_seq = seq // INPROJ_ROWS
    row_spec = lambda width: pl.BlockSpec((INPROJ_ROWS, width), lambda i: (i, 0))
    w_spec = pl.BlockSpec((d, n_in), lambda i: (0, 0), pipeline_mode=pl.Buffered(1))
    out_shape = [jax.ShapeDtypeStruct((m, width), BF16) for width in widths]
    out_specs = [row_spec(width) for width in widths]
    vmem = (d * n_in * 2 + 2 * INPROJ_ROWS * d * 4 + 2 * INPROJ_ROWS * n_in * 2
            + INPROJ_ROWS * d * 2 + 6 * 1024 * 1024)
    if scale is None:
        assert not convert
        return pl.pallas_call(
            functools.partial(_inproj_kernel_u, widths=widths),
            grid=(steps,),
            in_specs=[row_spec(d), w_spec],
            out_specs=out_specs,
            out_shape=out_shape,
            compiler_params=pltpu.CompilerParams(dimension_semantics=("arbitrary",),
                                                 vmem_limit_bytes=_vmem_limit(vmem)),
            name="inproj_u",
        )(x_or_u, w_in)

    slab_in_specs, slab_out_specs = [], []
    for a, layer in convert:
        _, r, c = a.shape
        slab = r // steps
        assert slab * steps == r and slab % BF16_SUBLANES == 0
        slab_in_specs.append(pl.BlockSpec((None, slab, c), lambda i, layer=layer: (layer, i, 0)))
        slab_out_specs.append(pl.BlockSpec((slab, c), lambda i: (i, 0)))
        out_shape.append(jax.ShapeDtypeStruct((r, c), BF16))
        vmem += 2 * slab * c * (4 + 2)
    mod_spec = pl.BlockSpec((None, 1, d), lambda i: (i // tiles_per_seq, 0, 0))
    return pl.pallas_call(
        functools.partial(_inproj_kernel_x, widths=widths, n_convert=len(convert)),
        grid=(steps,),
        in_specs=[row_spec(d), mod_spec, mod_spec, w_spec] + slab_in_specs,
        out_specs=out_specs + slab_out_specs,
        out_shape=out_shape,
        scratch_shapes=[pltpu.VMEM((INPROJ_ROWS, d), BF16)],
        compiler_params=pltpu.CompilerParams(dimension_semantics=("arbitrary",),
                                             vmem_limit_bytes=_vmem_limit(vmem)),
        name="inproj_x",
    )(x_or_u, scale, shift, w_in, *[a for a, _ in convert])


def _pool_mixer(pin_ref, pprev_ref, wgrp_ref, first_tile, tile_pos):
    rows = pin_ref.shape[0]
    p = pin_ref[...].astype(F32)
    prev = jnp.where(first_tile, 0.0, pprev_ref[...].astype(F32))
    ext = jnp.concatenate([prev, p], axis=0)
    pos = tile_pos + lax.broadcasted_iota(jnp.int32, (rows, 1), 0)
    mixed = []
    for g, w in enumerate(POOL_WINDOWS):
        lo, hi = g * POOL_GROUP_DIM, (g + 1) * POOL_GROUP_DIM
        s = ext[:, lo:hi]
        span = 1
        while span < w:
            s = s + pltpu.roll(s, span, 0)
            span *= 2
        win_sum = s[MAX_POOL_WINDOW:, :]
        count = jnp.minimum(pos + 1, w).astype(F32)
        pooled = win_sum / count - p[:, lo:hi]
        mixed.append(jnp.dot(pooled.astype(BF16), wgrp_ref[g * POOL_GROUP_DIM:
                                                            (g + 1) * POOL_GROUP_DIM, :],
                             preferred_element_type=F32))
    return jnp.concatenate(mixed, axis=1)


def _attention(q_ref, k_ref, kprev_ref, v_ref, vprev_ref, sinks_ref, first_tile):
    rows = q_ref.shape[0]
    stack = Q_PER_KV * ATTN_BLOCK
    kext = jnp.concatenate([kprev_ref[...], k_ref[...]], axis=0)
    vext = jnp.concatenate([vprev_ref[...], v_ref[...]], axis=0)
    kj = lax.broadcasted_iota(jnp.int32, (2 * ATTN_BLOCK, stack), 0)
    qi = lax.broadcasted_iota(jnp.int32, (2 * ATTN_BLOCK, stack), 1) & (ATTN_BLOCK - 1)
    band = (kj > qi) & (kj <= qi + ATTN_BLOCK)
    bias_inner = jnp.where(band, 0.0, -jnp.inf).astype(F32)
    bias_first = jnp.where(band & ((kj >= ATTN_BLOCK) | jnp.logical_not(first_tile)),
                           0.0, -jnp.inf).astype(F32)
    blocks = []
    for qb in range(rows // ATTN_BLOCK):
        r0 = qb * ATTN_BLOCK
        bias = bias_first if qb == 0 else bias_inner
        pieces = []
        for h in range(N_KV_HEADS):
            qh = q_ref[r0:r0 + ATTN_BLOCK, h * Q_PER_KV * HEAD_DIM:(h + 1) * Q_PER_KV * HEAD_DIM]
            qs = jnp.concatenate(
                [qh[:, g * HEAD_DIM:(g + 1) * HEAD_DIM] for g in range(Q_PER_KV)], axis=0)
            kh = kext[r0:r0 + 2 * ATTN_BLOCK, h * HEAD_DIM:(h + 1) * HEAD_DIM]
            vh = vext[r0:r0 + 2 * ATTN_BLOCK, h * HEAD_DIM:(h + 1) * HEAD_DIM]
            st = lax.dot_general(kh, qs, (((1,), (1,)), ((), ())),
                                 preferred_element_type=F32) + bias
            sink = LOG2_E * jnp.concatenate(
                [jnp.full((1, ATTN_BLOCK), sinks_ref[h * Q_PER_KV + g], F32)
                 for g in range(Q_PER_KV)], axis=1)
            mx = jnp.maximum(jnp.max(st, axis=0, keepdims=True), sink)
            e = jnp.exp2(st - mx)
            denom = jnp.sum(e, axis=0, keepdims=True) + jnp.exp2(sink - mx)
            probs = (e * (1.0 / denom)).astype(BF16)
            o = lax.dot_general(probs, vh, (((0,), (0,)), ((), ())),
                                preferred_element_type=F32)
            pieces.extend(o[g * ATTN_BLOCK:(g + 1) * ATTN_BLOCK] for g in range(Q_PER_KV))
        blocks.append(jnp.concatenate(pieces, axis=1))
    return jnp.concatenate(blocks, axis=0)


def _mix_kernel(sinks_ref, pin_ref, pprev_ref, pgate_ref, q_ref, k_ref, kprev_ref, v_ref,
                vprev_ref, agate_ref, g_ref, x_ref, gate_ref, wgrp_ref, pscale_ref,
                wpu_ref, wau_ref, wout_ref, lng_ref, lnb_ref, *rest, alpha, emit_next):
    if emit_next:
        nscale_ref, nshift_ref, xo_ref, uo_ref = rest
    else:
        (xo_ref,) = rest
    rows, d = x_ref.shape
    t = pl.program_id(1)
    first_tile = t == 0

    mixed = _pool_mixer(pin_ref, pprev_ref, wgrp_ref, first_tile, t * rows)
    y_pool = (mixed * pscale_ref[...] * pgate_ref[...].astype(F32)).astype(BF16)
    attn = _attention(q_ref, k_ref, kprev_ref, v_ref, vprev_ref, sinks_ref, first_tile)
    y_attn = (attn * agate_ref[...].astype(F32)).astype(BF16)

    up_pool = jnp.dot(y_pool, wpu_ref[...], preferred_element_type=F32)
    up_attn = jnp.dot(y_attn, wau_ref[...], preferred_element_type=F32)
    merged = (g_ref[:, :d].astype(F32) * up_pool + g_ref[:, d:].astype(F32) * up_attn)
    out = jnp.dot(merged.astype(BF16), wout_ref[...], preferred_element_type=F32)

    y = alpha * x_ref[...] + gate_ref[...] * out
    mu = jnp.mean(y, axis=-1, keepdims=True)
    yc = y - mu
    var = jnp.mean(yc * yc, axis=-1, keepdims=True)
    xn = yc * lax.rsqrt(var + LN_EPS) * lng_ref[...] + lnb_ref[...]
    xo_ref[...] = xn
    if emit_next:
        uo_ref[...] = (xn * (1.0 + nscale_ref[...]) + nshift_ref[...]).astype(BF16)


def _mix_call(h_parts, x, gate, next_mod, wgrp, pscale, sinks, wpu, wau, wout, lng, lnb,
              batch, seq, alpha):
    pin, pgate, q, k, v, agate, g = h_parts
    m, d = x.shape
    nt = seq // MIX_ROWS
    emit_next = next_mod is not None

    row = lambda width: pl.BlockSpec((MIX_ROWS, width), lambda b, t: (b * nt + t, 0))

    def prev_spec(block_rows, width):
        per_tile, per_seq = MIX_ROWS // block_rows, seq // block_rows
        return pl.BlockSpec(
            (block_rows, width),
            lambda b, t: (b * per_seq + jnp.maximum(t * per_tile - 1, 0), 0))

    full = lambda a: pl.BlockSpec(a.shape, lambda b, t: (0,) * a.ndim,
                                  pipeline_mode=pl.Buffered(1))
    mod_spec = pl.BlockSpec((None, 1, d), lambda b, t: (b, 0, 0))

    in_specs = [
        pl.BlockSpec(memory_space=pltpu.SMEM),
        row(pin.shape[1]), prev_spec(MAX_POOL_WINDOW, pin.shape[1]), row(pgate.shape[1]),
        row(q.shape[1]),
        row(k.shape[1]), prev_spec(ATTN_BLOCK, k.shape[1]),
        row(v.shape[1]), prev_spec(ATTN_BLOCK, v.shape[1]),
        row(agate.shape[1]), row(g.shape[1]), row(d), mod_spec,
        full(wgrp), full(pscale), full(wpu), full(wau), full(wout), full(lng), full(lnb),
    ]
    args = [sinks, pin, pin, pgate, q, k, k, v, v, agate, g, x, gate,
            wgrp, pscale, wpu, wau, wout, lng, lnb]
    out_shape = [jax.ShapeDtypeStruct((m, d), F32)]
    out_specs = [row(d)]
    if emit_next:
        in_specs += [mod_spec, mod_spec]
        args += list(next_mod)
        out_shape.append(jax.ShapeDtypeStruct((m, d), BF16))
        out_specs.append(row(d))

    h_cols = sum(a.shape[1] for a in h_parts)
    weights = 2 * (wgrp.size + wpu.size + wau.size + wout.size)
    tiles = 2 * MIX_ROWS * (h_cols * 2 + d * 4 + d * 4 + d * 2)
    temps = 10 * MIX_ROWS * d * 4
    return pl.pallas_call(
        functools.partial(_mix_kernel, alpha=alpha, emit_next=emit_next),
        grid=(batch, nt),
        in_specs=in_specs,
        out_specs=out_specs,
        out_shape=out_shape,
        compiler_params=pltpu.CompilerParams(
            dimension_semantics=("arbitrary", "arbitrary"),
            vmem_limit_bytes=_vmem_limit(weights + tiles + temps),
        ),
        name="mixer",
    )(*args)


def kernel(x, c, w_ada, b_ada, w_in, w_pool_grp, pool_scale, sinks, w_pool_up, w_attn_up,
           w_out, ln_g, ln_b):
    batch, seq, d = x.shape
    depth = w_ada.shape[0]
    pool_width = w_pool_grp.shape[1] * w_pool_grp.shape[2]
    attn_width = w_attn_up.shape[1]
    kv_width = N_KV_HEADS * HEAD_DIM
    widths = (pool_width, pool_width, attn_width, kv_width, kv_width, attn_width, 2 * d)
    assert sum(widths) == w_in.shape[2]
    assert seq % MIX_ROWS == 0 and seq % INPROJ_ROWS == 0 and MIX_ROWS % ATTN_BLOCK == 0
    alpha = (2 * depth) ** 0.25

    mod = _ada_call(c, w_ada, b_ada)
    shift, scale, gate = [mod[:, :, i * d:(i + 1) * d].reshape(depth, batch, 1, d)
                          for i in range(3)]

    stacks = [w_in, w_pool_grp.reshape(depth, pool_width, -1), w_pool_up, w_attn_up, w_out]
    converted = {0: [w[0].astype(BF16) for w in stacks]}
    later = [(w, l) for l in range(1, depth) for w in stacks]

    xf = x.reshape(batch * seq, d)
    u = None
    for l in range(depth):
        w_in_l, wgrp_l, wpu_l, wau_l, wout_l = converted[l]
        if l == 0:
            outs = _inproj_call(xf, scale[l], shift[l], w_in_l, widths, seq, convert=later)
            h_parts, later_b = outs[:len(widths)], outs[len(widths):]
            for j in range(1, depth):
                converted[j] = later_b[(j - 1) * 5:j * 5]
        else:
            h_parts = _inproj_call(u, None, None, w_in_l, widths, seq)
        next_mod = (scale[l + 1], shift[l + 1]) if l + 1 < depth else None
        outs = _mix_call(
            h_parts, xf, gate[l], next_mod, wgrp_l, pool_scale[l].reshape(1, -1), sinks[l],
            wpu_l, wau_l, wout_l, ln_g[l].reshape(1, d), ln_b[l].reshape(1, d),
            batch, seq, alpha)
        xf = outs[0]
        if next_mod is not None:
            u = outs[1]
    return xf.reshape(batch, seq, d)
```

```python
import functools

import jax
import jax.numpy as jnp
from jax import lax
from jax.experimental import pallas as pl
from jax.experimental.pallas import tpu as pltpu

F32 = jnp.float32
BF16 = jnp.bfloat16

POOL_WINDOWS = (2, 4, 8, 16)
POOL_GROUP_DIM = 256
MAX_POOL_WINDOW = 16
HEAD_DIM = 64
N_KV_HEADS = 4
Q_PER_KV = 4
ATTN_BLOCK = 128
LN_EPS = 1e-5
LOG2_E = 1.4426950408889634

V7X_VMEM_LIMIT_CAP = 60000 * 1024
BF16_SUBLANES = 16
LANES = 128

ADA_COLS = 1536
INPROJ_ROWS = 256
INPROJ_COLS = 512
MIX_ROWS = 256
MIX_OUT_COLS = 512


def _vmem_limit(estimate_bytes):
    return int(min(V7X_VMEM_LIMIT_CAP, max(32 * 1024 * 1024, estimate_bytes)))


def _ada_kernel(c_ref, w_ref, b_ref, o_ref):
    sc = jax.nn.silu(c_ref[...]).astype(BF16)
    acc = jnp.dot(sc, w_ref[...].astype(BF16), preferred_element_type=F32)
    o_ref[...] = acc + b_ref[...]


def _ada_call(c, w_ada, b_ada):
    depth, d, n = w_ada.shape
    b = c.shape[0]
    return pl.pallas_call(
        _ada_kernel,
        grid=(depth, n // ADA_COLS),
        in_specs=[
            pl.BlockSpec((b, d), lambda l, j: (0, 0)),
            pl.BlockSpec((None, d, ADA_COLS), lambda l, j: (l, 0, j)),
            pl.BlockSpec((None, 1, ADA_COLS), lambda l, j: (l, 0, j)),
        ],
        out_specs=pl.BlockSpec((None, b, ADA_COLS), lambda l, j: (l, 0, j)),
        out_shape=jax.ShapeDtypeStruct((depth, b, n), F32),
        compiler_params=pltpu.CompilerParams(
            dimension_semantics=("arbitrary", "arbitrary"),
            vmem_limit_bytes=_vmem_limit(2 * d * ADA_COLS * 4 + 8 * 1024 * 1024),
        ),
        name="ada_mod",
    )(c, w_ada, b_ada.reshape(depth, 1, n))


def _pool_group(g, p, prev, wgrp_ref, pos):
    w = POOL_WINDOWS[g]
    s = jnp.concatenate([prev, p], axis=0)
    span = 1
    while span < w:
        s = s + pltpu.roll(s, span, 0)
        span *= 2
    win_sum = s[MAX_POOL_WINDOW:, :]
    count = jnp.minimum(pos + 1, w).astype(F32)
    pooled = win_sum / count - p
    return jnp.dot(pooled.astype(BF16),
                   wgrp_ref[g * POOL_GROUP_DIM:(g + 1) * POOL_GROUP_DIM, :],
                   preferred_element_type=F32)


def _inproj_kernel(x_ref, scale_ref, shift_ref, w_ref, wgrp_ref, pscale_ref, *rest,
                   widths, n_convert, tiles_per_seq):
    slabs_in, rest = rest[:n_convert], rest[n_convert:]
    ypool_ref, q_ref, k_ref, v_ref, agate_ref, g_ref = rest[:6]
    slabs_out = rest[6:6 + n_convert]
    u_scr, mixed_scr, tail_scr = rest[6 + n_convert:]
    rows = x_ref.shape[0]
    i = pl.program_id(0)
    t = i % tiles_per_seq
    first_tile = t == 0

    @pl.when(i == 0)
    def _():
        tail_scr[...] = jnp.zeros(tail_scr.shape, tail_scr.dtype)

    u_scr[...] = (x_ref[...] * (1.0 + scale_ref[...]) + shift_ref[...]).astype(BF16)

    def project(c0, c1):
        return jnp.dot(u_scr[...], w_ref[:, c0:c1], preferred_element_type=F32)

    pool_width = widths[0]
    pos = t * rows + lax.broadcasted_iota(jnp.int32, (rows, 1), 0)
    groups_per_chunk = INPROJ_COLS // POOL_GROUP_DIM
    for c in range(0, pool_width, INPROJ_COLS):
        acc = project(c, c + INPROJ_COLS)
        for j in range(groups_per_chunk):
            lo = c + j * POOL_GROUP_DIM
            p = acc[:, j * POOL_GROUP_DIM:(j + 1) * POOL_GROUP_DIM]
            prev = jnp.where(first_tile, 0.0, tail_scr[:, lo:lo + POOL_GROUP_DIM])
            tail_scr[:, lo:lo + POOL_GROUP_DIM] = p[rows - MAX_POOL_WINDOW:, :]
            mixed_scr[:, lo:lo + POOL_GROUP_DIM] = _pool_group(
                lo // POOL_GROUP_DIM, p, prev, wgrp_ref, pos)
    col = pool_width
    for c in range(0, pool_width, INPROJ_COLS):
        acc = project(col + c, col + c + INPROJ_COLS)
        ypool_ref[:, c:c + INPROJ_COLS] = (
            mixed_scr[:, c:c + INPROJ_COLS] * pscale_ref[:, c:c + INPROJ_COLS]
            * jax.nn.silu(acc)).astype(BF16)
    col += pool_width

    epilogues = (
        (q_ref, lambda a: a * (HEAD_DIM ** -0.5 * LOG2_E)),
        (k_ref, None),
        (v_ref, None),
        (agate_ref, jax.nn.silu),
        (g_ref, jax.nn.sigmoid),
    )
    for (o_ref, fn), width in zip(epilogues, widths[2:]):
        step = min(INPROJ_COLS, width)
        for c in range(0, width, step):
            acc = project(col + c, col + c + step)
            if fn is not None:
                acc = fn(acc)
            o_ref[:, c:c + step] = acc.astype(o_ref.dtype)
        col += width

    for src, dst in zip(slabs_in, slabs_out):
        dst[...] = src[...].astype(BF16)


def _inproj_call(x, scale, shift, w_in, wgrp, pscale, widths, seq, convert=()):
    m, d = x.shape
    n_in = w_in.shape[1]
    steps = m // INPROJ_ROWS
    tiles_per_seq = seq // INPROJ_ROWS
    pool_width = widths[0]
    assert widths[1] == pool_width and pool_width % INPROJ_COLS == 0
    row_spec = lambda width: pl.BlockSpec((INPROJ_ROWS, width), lambda i: (i, 0))
    full = lambda a: pl.BlockSpec(a.shape, lambda i: (0,) * a.ndim,
                                  pipeline_mode=pl.Buffered(1))
    mod_spec = pl.BlockSpec((None, 1, d), lambda i: (i // tiles_per_seq, 0, 0))
    out_widths = (pool_width,) + tuple(widths[2:])
    out_shape = [jax.ShapeDtypeStruct((m, width), BF16) for width in out_widths]
    out_specs = [row_spec(width) for width in out_widths]
    vmem = (d * n_in * 2 + 2 * INPROJ_ROWS * d * 4 + 2 * INPROJ_ROWS * sum(out_widths) * 2
            + INPROJ_ROWS * d * 2 + INPROJ_ROWS * pool_width * 4 + 8 * 1024 * 1024)
    slab_in_specs, slab_out_specs = [], []
    for a, layer in convert:
        _, r, c = a.shape
        slab = r // steps
        assert slab * steps == r and slab % BF16_SUBLANES == 0
        slab_in_specs.append(pl.BlockSpec((None, slab, c), lambda i, layer=layer: (layer, i, 0)))
        slab_out_specs.append(pl.BlockSpec((slab, c), lambda i: (i, 0)))
        out_shape.append(jax.ShapeDtypeStruct((r, c), BF16))
        vmem += 2 * slab * c * (4 + 2)
    return pl.pallas_call(
        functools.partial(_inproj_kernel, widths=widths, n_convert=len(convert),
                          tiles_per_seq=tiles_per_seq),
        grid=(steps,),
        in_specs=[row_spec(d), mod_spec, mod_spec, full(w_in), full(wgrp), full(pscale)]
        + slab_in_specs,
        out_specs=out_specs + slab_out_specs,
        out_shape=out_shape,
        scratch_shapes=[pltpu.VMEM((INPROJ_ROWS, d), BF16),
                        pltpu.VMEM((INPROJ_ROWS, pool_width), F32),
                        pltpu.VMEM((MAX_POOL_WINDOW, pool_width), F32)],
        compiler_params=pltpu.CompilerParams(dimension_semantics=("arbitrary",),
                                             vmem_limit_bytes=_vmem_limit(vmem)),
        name="inproj",
    )(x, scale, shift, w_in, wgrp, pscale, *[a for a, _ in convert])


def _attention(q_ref, k_ref, kprev_ref, v_ref, vprev_ref, sinks_ref, first_tile):
    rows = q_ref.shape[0]
    stack = Q_PER_KV * ATTN_BLOCK
    kext = jnp.concatenate([kprev_ref[...], k_ref[...]], axis=0)
    vext = jnp.concatenate([vprev_ref[...], v_ref[...]], axis=0)
    kj = lax.broadcasted_iota(jnp.int32, (2 * ATTN_BLOCK, stack), 0)
    qi = lax.broadcasted_iota(jnp.int32, (2 * ATTN_BLOCK, stack), 1) & (ATTN_BLOCK - 1)
    band = (kj > qi) & (kj <= qi + ATTN_BLOCK)
    bias_inner = jnp.where(band, 0.0, -jnp.inf).astype(F32)
    bias_first = jnp.where(band & ((kj >= ATTN_BLOCK) | jnp.logical_not(first_tile)),
                           0.0, -jnp.inf).astype(F32)
    blocks = []
    for qb in range(rows // ATTN_BLOCK):
        r0 = qb * ATTN_BLOCK
        bias = bias_first if qb == 0 else bias_inner
        pieces = []
        for h in range(N_KV_HEADS):
            qh = q_ref[r0:r0 + ATTN_BLOCK, h * Q_PER_KV * HEAD_DIM:(h + 1) * Q_PER_KV * HEAD_DIM]
            qs = jnp.concatenate(
                [qh[:, g * HEAD_DIM:(g + 1) * HEAD_DIM] for g in range(Q_PER_KV)], axis=0)
            kh = kext[r0:r0 + 2 * ATTN_BLOCK, h * HEAD_DIM:(h + 1) * HEAD_DIM]
            vh = vext[r0:r0 + 2 * ATTN_BLOCK, h * HEAD_DIM:(h + 1) * HEAD_DIM]
            st = lax.dot_general(kh, qs, (((1,), (1,)), ((), ())),
                                 preferred_element_type=F32) + bias
            sink = LOG2_E * jnp.concatenate(
                [jnp.full((1, ATTN_BLOCK), sinks_ref[h * Q_PER_KV + g], F32)
                 for g in range(Q_PER_KV)], axis=1)
            mx = jnp.maximum(jnp.max(st, axis=0, keepdims=True), sink)
            e = jnp.exp2(st - mx)
            denom = jnp.sum(e, axis=0, keepdims=True) + jnp.exp2(sink - mx)
            probs = (e * (1.0 / denom)).astype(BF16)
            o = lax.dot_general(probs, vh, (((0,), (0,)), ((), ())),
                                preferred_element_type=F32)
            pieces.extend(o[g * ATTN_BLOCK:(g + 1) * ATTN_BLOCK] for g in range(Q_PER_KV))
        blocks.append(jnp.concatenate(pieces, axis=1))
    return jnp.concatenate(blocks, axis=0)


def _mix_kernel(sinks_ref, ypool_ref, q_ref, k_ref, kprev_ref, v_ref, vprev_ref, agate_ref,
                g_ref, x_ref, gate_ref, wpu_ref, wau_ref, wout_ref, lng_ref, lnb_ref, xo_ref,
                merged_scr, cen_scr, *, alpha):
    rows, d = x_ref.shape
    first_tile = pl.program_id(1) == 0

    attn = _attention(q_ref, k_ref, kprev_ref, v_ref, vprev_ref, sinks_ref, first_tile)
    y_attn = (attn * agate_ref[...].astype(F32)).astype(BF16)

    up_pool = jnp.dot(ypool_ref[...], wpu_ref[...], preferred_element_type=F32)
    up_attn = jnp.dot(y_attn, wau_ref[...], preferred_element_type=F32)
    merged_scr[...] = (g_ref[:, :d].astype(F32) * up_pool
                       + g_ref[:, d:].astype(F32) * up_attn).astype(BF16)

    lanes = LANES
    s1 = jnp.zeros((rows, lanes), F32)
    s2 = jnp.zeros((rows, lanes), F32)
    shift = None
    for lo in range(0, d, MIX_OUT_COLS):
        hi = lo + MIX_OUT_COLS
        out = jnp.dot(merged_scr[...], wout_ref[:, lo:hi], preferred_element_type=F32)
        y = alpha * x_ref[:, lo:hi] + gate_ref[:, lo:hi] * out
        if shift is None:
            shift = jnp.mean(y, axis=-1, keepdims=True)
        cen = y - shift
        cen_scr[:, lo:hi] = cen
        for j in range(0, MIX_OUT_COLS, lanes):
            piece = cen[:, j:j + lanes]
            s1 = s1 + piece
            s2 = s2 + piece * piece
    mu = jnp.sum(s1, axis=-1, keepdims=True) * (1.0 / d)
    var = jnp.sum(s2, axis=-1, keepdims=True) * (1.0 / d) - mu * mu
    rstd = lax.rsqrt(var + LN_EPS)
    xo_ref[...] = (cen_scr[...] - mu) * rstd * lng_ref[...] + lnb_ref[...]


def _mix_call(h_parts, x, gate, sinks, wpu, wau, wout, lng, lnb, batch, seq, alpha):
    ypool, q, k, v, agate, g = h_parts
    m, d = x.shape
    nt = seq // MIX_ROWS

    row = lambda width: pl.BlockSpec((MIX_ROWS, width), lambda b, t: (b * nt + t, 0))

    def prev_spec(width):
        per_tile, per_seq = MIX_ROWS // ATTN_BLOCK, seq // ATTN_BLOCK
        return pl.BlockSpec(
            (ATTN_BLOCK, width),
            lambda b, t: (b * per_seq + jnp.maximum(t * per_tile - 1, 0), 0))

    full = lambda a: pl.BlockSpec(a.shape, lambda b, t: (0,) * a.ndim,
                                  pipeline_mode=pl.Buffered(1))
    mod_spec = pl.BlockSpec((None, 1, d), lambda b, t: (b, 0, 0))

    in_specs = [
        pl.BlockSpec(memory_space=pltpu.SMEM),
        row(ypool.shape[1]), row(q.shape[1]),
        row(k.shape[1]), prev_spec(k.shape[1]),
        row(v.shape[1]), prev_spec(v.shape[1]),
        row(agate.shape[1]), row(g.shape[1]), row(d), mod_spec,
        full(wpu), full(wau), full(wout), full(lng), full(lnb),
    ]
    args = [sinks, ypool, q, k, k, v, v, agate, g, x, gate, wpu, wau, wout, lng, lnb]

    h_cols = sum(a.shape[1] for a in h_parts)
    weights = 2 * (wpu.size + wau.size + wout.size)
    tiles = 2 * MIX_ROWS * (h_cols * 2 + d * 4 + d * 4)
    temps = 10 * MIX_ROWS * d * 4
    return pl.pallas_call(
        functools.partial(_mix_kernel, alpha=alpha),
        grid=(batch, nt),
        in_specs=in_specs,
        out_specs=row(d),
        out_shape=jax.ShapeDtypeStruct((m, d), F32),
        scratch_shapes=[pltpu.VMEM((MIX_ROWS, d), BF16),
                        pltpu.VMEM((MIX_ROWS, d), F32)],
        compiler_params=pltpu.CompilerParams(
            dimension_semantics=("arbitrary", "arbitrary"),
            vmem_limit_bytes=_vmem_limit(weights + tiles + temps),
        ),
        name="mixer",
    )(*args)


def kernel(x, c, w_ada, b_ada, w_in, w_pool_grp, pool_scale, sinks, w_pool_up, w_attn_up,
           w_out, ln_g, ln_b):
    batch, seq, d = x.shape
    depth = w_ada.shape[0]
    pool_width = w_pool_grp.shape[1] * w_pool_grp.shape[2]
    attn_width = w_attn_up.shape[1]
    kv_width = N_KV_HEADS * HEAD_DIM
    widths = (pool_width, pool_width, attn_width, kv_width, kv_width, attn_width, 2 * d)
    assert sum(widths) == w_in.shape[2]
    assert seq % MIX_ROWS == 0 and seq % INPROJ_ROWS == 0 and MIX_ROWS % ATTN_BLOCK == 0
    alpha = (2 * depth) ** 0.25

    mod = _ada_call(c, w_ada, b_ada)
    shift, scale, gate = [mod[:, :, i * d:(i + 1) * d].reshape(depth, batch, 1, d)
                          for i in range(3)]

    stacks = {"in": w_in, "grp": w_pool_grp.reshape(depth, pool_width, -1),
              "pool_up": w_pool_up, "attn_up": w_attn_up, "out": w_out}
    bf16_w = {("in", 0): w_in[0].astype(BF16), ("grp", 0): stacks["grp"][0].astype(BF16)}
    pending = [(name, l) for l in range(depth) for name in stacks if (name, l) not in bf16_w]

    xf = x.reshape(batch * seq, d)
    for l in range(depth):
        convert = [(stacks[name], j) for name, j in pending] if l == 0 else ()
        outs = _inproj_call(xf, scale[l], shift[l], bf16_w["in", l], bf16_w["grp", l],
                            pool_scale[l].reshape(1, -1), widths, seq, convert=convert)
        h_parts = outs[:6]
        if l == 0:
            bf16_w.update(zip(pending, outs[6:]))
        xf = _mix_call(h_parts, xf, gate[l], sinks[l], bf16_w["pool_up", l],
                       bf16_w["attn_up", l], bf16_w["out", l],
                       ln_g[l].reshape(1, d), ln_b[l].reshape(1, d), batch, seq, alpha)
    return xf.reshape(batch, seq, d)
```

```python
import functools

import jax
import jax.numpy as jnp
from jax import lax
from jax.experimental import pallas as pl
from jax.experimental.pallas import tpu as pltpu

F32 = jnp.float32
BF16 = jnp.bfloat16

POOL_WINDOWS = (2, 4, 8, 16)
POOL_GROUP_DIM = 256
MAX_POOL_WINDOW = 16
HEAD_DIM = 64
N_KV_HEADS = 4
Q_PER_KV = 4
ATTN_BLOCK = 128
LN_EPS = 1e-5
LOG2_E = 1.4426950408889634

V7X_VMEM_LIMIT_CAP = 60000 * 1024
BF16_SUBLANES = 16
LANES = 128

ADA_COLS = 1536
INPROJ_ROWS = 256
INPROJ_COLS = 512
MIX_ROWS = 512
MIX_OUT_COLS = 512


def _vmem_limit(estimate_bytes):
    return int(min(V7X_VMEM_LIMIT_CAP, max(32 * 1024 * 1024, estimate_bytes)))


def _ada_kernel(c_ref, w_ref, b_ref, o_ref):
    sc = jax.nn.silu(c_ref[...]).astype(BF16)
    acc = jnp.dot(sc, w_ref[...].astype(BF16), preferred_element_type=F32)
    o_ref[...] = acc + b_ref[...]


def _ada_call(c, w_ada, b_ada):
    depth, d, n = w_ada.shape
    b = c.shape[0]
    return pl.pallas_call(
        _ada_kernel,
        grid=(depth, n // ADA_COLS),
        in_specs=[
            pl.BlockSpec((b, d), lambda l, j: (0, 0)),
            pl.BlockSpec((None, d, ADA_COLS), lambda l, j: (l, 0, j)),
            pl.BlockSpec((None, 1, ADA_COLS), lambda l, j: (l, 0, j)),
        ],
        out_specs=pl.BlockSpec((None, b, ADA_COLS), lambda l, j: (l, 0, j)),
        out_shape=jax.ShapeDtypeStruct((depth, b, n), F32),
        compiler_params=pltpu.CompilerParams(
            dimension_semantics=("arbitrary", "arbitrary"),
            vmem_limit_bytes=_vmem_limit(2 * d * ADA_COLS * 4 + 8 * 1024 * 1024),
        ),
        name="ada_mod",
    )(c, w_ada, b_ada.reshape(depth, 1, n))


def _pool_group(g, p, prev, wgrp_ref, pos):
    w = POOL_WINDOWS[g]
    s = jnp.concatenate([prev, p], axis=0)
    span = 1
    while span < w:
        s = s + pltpu.roll(s, span, 0)
        span *= 2
    win_sum = s[MAX_POOL_WINDOW:, :]
    count = jnp.minimum(pos + 1, w).astype(F32)
    pooled = win_sum / count - p
    return jnp.dot(pooled.astype(BF16),
                   wgrp_ref[g * POOL_GROUP_DIM:(g + 1) * POOL_GROUP_DIM, :],
                   preferred_element_type=F32)


def _inproj_kernel(x_ref, scale_ref, shift_ref, w_ref, wgrp_ref, pscale_ref, *rest,
                   widths, n_convert, tiles_per_seq):
    slabs_in, rest = rest[:n_convert], rest[n_convert:]
    ypool_ref, q_ref, k_ref, v_ref, agate_ref, g_ref = rest[:6]
    slabs_out = rest[6:6 + n_convert]
    u_scr, mixed_scr, tail_scr = rest[6 + n_convert:]
    rows = x_ref.shape[0]
    i = pl.program_id(0)
    t = i % tiles_per_seq
    first_tile = t == 0

    @pl.when(i == 0)
    def _():
        tail_scr[...] = jnp.zeros(tail_scr.shape, tail_scr.dtype)

    u_scr[...] = (x_ref[...] * (1.0 + scale_ref[...]) + shift_ref[...]).astype(BF16)

    def project(c0, c1):
        return jnp.dot(u_scr[...], w_ref[:, c0:c1], preferred_element_type=F32)

    pool_width = widths[0]
    pos = t * rows + lax.broadcasted_iota(jnp.int32, (rows, 1), 0)
    groups_per_chunk = INPROJ_COLS // POOL_GROUP_DIM
    for c in range(0, pool_width, INPROJ_COLS):
        acc = project(c, c + INPROJ_COLS)
        for j in range(groups_per_chunk):
            lo = c + j * POOL_GROUP_DIM
            p = acc[:, j * POOL_GROUP_DIM:(j + 1) * POOL_GROUP_DIM]
            prev = jnp.where(first_tile, 0.0, tail_scr[:, lo:lo + POOL_GROUP_DIM])
            tail_scr[:, lo:lo + POOL_GROUP_DIM] = p[rows - MAX_POOL_WINDOW:, :]
            mixed_scr[:, lo:lo + POOL_GROUP_DIM] = _pool_group(
                lo // POOL_GROUP_DIM, p, prev, wgrp_ref, pos)
    col = pool_width
    for c in range(0, pool_width, INPROJ_COLS):
        acc = project(col + c, col + c + INPROJ_COLS)
        ypool_ref[:, c:c + INPROJ_COLS] = (
            mixed_scr[:, c:c + INPROJ_COLS] * pscale_ref[:, c:c + INPROJ_COLS]
            * jax.nn.silu(acc)).astype(BF16)
    col += pool_width

    epilogues = (
        (q_ref, lambda a: a * (HEAD_DIM ** -0.5 * LOG2_E)),
        (k_ref, None),
        (v_ref, None),
        (agate_ref, jax.nn.silu),
        (g_ref, jax.nn.sigmoid),
    )
    for (o_ref, fn), width in zip(epilogues, widths[2:]):
        step = min(INPROJ_COLS, width)
        for c in range(0, width, step):
            acc = project(col + c, col + c + step)
            if fn is not None:
                acc = fn(acc)
            o_ref[:, c:c + step] = acc.astype(o_ref.dtype)
        col += width

    for src, dst in zip(slabs_in, slabs_out):
        dst[...] = src[...].astype(BF16)


def _inproj_call(x, scale, shift, w_in, wgrp, pscale, widths, seq, convert=()):
    m, d = x.shape
    n_in = w_in.shape[1]
    steps = m // INPROJ_ROWS
    tiles_per_seq = seq // INPROJ_ROWS
    pool_width = widths[0]
    assert widths[1] == pool_width and pool_width % INPROJ_COLS == 0
    row_spec = lambda width: pl.BlockSpec((INPROJ_ROWS, width), lambda i: (i, 0))
    full = lambda a: pl.BlockSpec(a.shape, lambda i: (0,) * a.ndim,
                                  pipeline_mode=pl.Buffered(1))
    mod_spec = pl.BlockSpec((None, 1, d), lambda i: (i // tiles_per_seq, 0, 0))
    out_widths = (pool_width,) + tuple(widths[2:])
    out_shape = [jax.ShapeDtypeStruct((m, width), BF16) for width in out_widths]
    out_specs = [row_spec(width) for width in out_widths]
    vmem = (d * n_in * 2 + 2 * INPROJ_ROWS * d * 4 + 2 * INPROJ_ROWS * sum(out_widths) * 2
            + INPROJ_ROWS * d * 2 + INPROJ_ROWS * pool_width * 4 + 8 * 1024 * 1024)
    slab_in_specs, slab_out_specs = [], []
    for a, layer in convert:
        _, r, c = a.shape
        slab = r // steps
        assert slab * steps == r and slab % BF16_SUBLANES == 0
        slab_in_specs.append(pl.BlockSpec((None, slab, c), lambda i, layer=layer: (layer, i, 0)))
        slab_out_specs.append(pl.BlockSpec((slab, c), lambda i: (i, 0)))
        out_shape.append(jax.ShapeDtypeStruct((r, c), BF16))
        vmem += 2 * slab * c * (4 + 2)
    return pl.pallas_call(
        functools.partial(_inproj_kernel, widths=widths, n_convert=len(convert),
                          tiles_per_seq=tiles_per_seq),
        grid=(steps,),
        in_specs=[row_spec(d), mod_spec, mod_spec, full(w_in), full(wgrp), full(pscale)]
        + slab_in_specs,
        out_specs=out_specs + slab_out_specs,
        out_shape=out_shape,
        scratch_shapes=[pltpu.VMEM((INPROJ_ROWS, d), BF16),
                        pltpu.VMEM((INPROJ_ROWS, pool_width), F32),
                        pltpu.VMEM((MAX_POOL_WINDOW, pool_width), F32)],
        compiler_params=pltpu.CompilerParams(dimension_semantics=("arbitrary",),
                                             vmem_limit_bytes=_vmem_limit(vmem)),
        name="inproj",
    )(x, scale, shift, w_in, wgrp, pscale, *[a for a, _ in convert])


def _attention(q_ref, k_ref, kprev_ref, v_ref, vprev_ref, sinks_ref, first_tile):
    rows = q_ref.shape[0]
    stack = Q_PER_KV * ATTN_BLOCK
    kext = jnp.concatenate([kprev_ref[...], k_ref[...]], axis=0)
    vext = jnp.concatenate([vprev_ref[...], v_ref[...]], axis=0)
    kj = lax.broadcasted_iota(jnp.int32, (2 * ATTN_BLOCK, stack), 0)
    qi = lax.broadcasted_iota(jnp.int32, (2 * ATTN_BLOCK, stack), 1) & (ATTN_BLOCK - 1)
    band = (kj > qi) & (kj <= qi + ATTN_BLOCK)
    bias_inner = jnp.where(band, 0.0, -jnp.inf).astype(F32)
    bias_first = jnp.where(band & ((kj >= ATTN_BLOCK) | jnp.logical_not(first_tile)),
                           0.0, -jnp.inf).astype(F32)
    blocks = []
    for qb in range(rows // ATTN_BLOCK):
        r0 = qb * ATTN_BLOCK
        bias = bias_first if qb == 0 else bias_inner
        pieces = []
        for h in range(N_KV_HEADS):
            qh = q_ref[r0:r0 + ATTN_BLOCK, h * Q_PER_KV * HEAD_DIM:(h + 1) * Q_PER_KV * HEAD_DIM]
            qs = jnp.concatenate(
                [qh[:, g * HEAD_DIM:(g + 1) * HEAD_DIM] for g in range(Q_PER_KV)], axis=0)
            kh = kext[r0:r0 + 2 * ATTN_BLOCK, h * HEAD_DIM:(h + 1) * HEAD_DIM]
            vh = vext[r0:r0 + 2 * ATTN_BLOCK, h * HEAD_DIM:(h + 1) * HEAD_DIM]
            st = lax.dot_general(kh, qs, (((1,), (1,)), ((), ())),
                                 preferred_element_type=F32) + bias
            sink = LOG2_E * jnp.concatenate(
                [jnp.full((1, ATTN_BLOCK), sinks_ref[h * Q_PER_KV + g], F32)
                 for g in range(Q_PER_KV)], axis=1)
            mx = jnp.maximum(jnp.max(st, axis=0, keepdims=True), sink)
            e = jnp.exp2(st - mx)
            denom = jnp.sum(e, axis=0, keepdims=True) + jnp.exp2(sink - mx)
            probs = (e * (1.0 / denom)).astype(BF16)
            o = lax.dot_general(probs, vh, (((0,), (0,)), ((), ())),
                                preferred_element_type=F32)
            pieces.extend(o[g * ATTN_BLOCK:(g + 1) * ATTN_BLOCK] for g in range(Q_PER_KV))
        blocks.append(jnp.concatenate(pieces, axis=1))
    return jnp.concatenate(blocks, axis=0)


def _mix_kernel(sinks_ref, ypool_ref, q_ref, k_ref, kprev_ref, v_ref, vprev_ref, agate_ref,
                g_ref, x_ref, gate_ref, wpu_ref, wau_ref, wout_ref, lng_ref, lnb_ref, xo_ref,
                yattn_scr, merged_scr, cen_scr, *, alpha):
    rows, d = x_ref.shape
    first_tile = pl.program_id(1) == 0

    attn = _attention(q_ref, k_ref, kprev_ref, v_ref, vprev_ref, sinks_ref, first_tile)
    y_attn = (attn * agate_ref[...].astype(F32)).astype(BF16)

    yattn_scr[...] = y_attn
    for lo in range(0, d, MIX_OUT_COLS):
        hi = lo + MIX_OUT_COLS
        up_pool = jnp.dot(ypool_ref[...], wpu_ref[:, lo:hi], preferred_element_type=F32)
        up_attn = jnp.dot(yattn_scr[...], wau_ref[:, lo:hi], preferred_element_type=F32)
        merged_scr[:, lo:hi] = (g_ref[:, lo:hi].astype(F32) * up_pool
                                + g_ref[:, d + lo:d + hi].astype(F32) * up_attn).astype(BF16)

    lanes = LANES
    s1 = jnp.zeros((rows, lanes), F32)
    s2 = jnp.zeros((rows, lanes), F32)
    shift = None
    for lo in range(0, d, MIX_OUT_COLS):
        hi = lo + MIX_OUT_COLS
        out = jnp.dot(merged_scr[...], wout_ref[:, lo:hi], preferred_element_type=F32)
        y = alpha * x_ref[:, lo:hi] + gate_ref[:, lo:hi] * out
        if shift is None:
            shift = jnp.mean(y, axis=-1, keepdims=True)
        cen = y - shift
        cen_scr[:, lo:hi] = cen
        for j in range(0, MIX_OUT_COLS, lanes):
            piece = cen[:, j:j + lanes]
            s1 = s1 + piece
            s2 = s2 + piece * piece
    mu = jnp.sum(s1, axis=-1, keepdims=True) * (1.0 / d)
    var = jnp.sum(s2, axis=-1, keepdims=True) * (1.0 / d) - mu * mu
    rstd = lax.rsqrt(var + LN_EPS)
    xo_ref[...] = (cen_scr[...] - mu) * rstd * lng_ref[...] + lnb_ref[...]


def _mix_call(h_parts, x, gate, sinks, wpu, wau, wout, lng, lnb, batch, seq, alpha):
    ypool, q, k, v, agate, g = h_parts
    m, d = x.shape
    nt = seq // MIX_ROWS

    row = lambda width: pl.BlockSpec((MIX_ROWS, width), lambda b, t: (b * nt + t, 0))

    def prev_spec(width):
        per_tile, per_seq = MIX_ROWS // ATTN_BLOCK, seq // ATTN_BLOCK
        return pl.BlockSpec(
            (ATTN_BLOCK, width),
            lambda b, t: (b * per_seq + jnp.maximum(t * per_tile - 1, 0), 0))

    full = lambda a: pl.BlockSpec(a.shape, lambda b, t: (0,) * a.ndim,
                                  pipeline_mode=pl.Buffered(1))
    mod_spec = pl.BlockSpec((None, 1, d), lambda b, t: (b, 0, 0))

    in_specs = [
        pl.BlockSpec(memory_space=pltpu.SMEM),
        row(ypool.shape[1]), row(q.shape[1]),
        row(k.shape[1]), prev_spec(k.shape[1]),
        row(v.shape[1]), prev_spec(v.shape[1]),
        row(agate.shape[1]), row(g.shape[1]), row(d), mod_spec,
        full(wpu), full(wau), full(wout), full(lng), full(lnb),
    ]
    args = [sinks, ypool, q, k, k, v, v, agate, g, x, gate, wpu, wau, wout, lng, lnb]

    h_cols = sum(a.shape[1] for a in h_parts)
    weights = 2 * (wpu.size + wau.size + wout.size)
    tiles = 2 * MIX_ROWS * (h_cols * 2 + d * 4 + d * 4)
    temps = 10 * MIX_ROWS * d * 4
    return pl.pallas_call(
        functools.partial(_mix_kernel, alpha=alpha),
        grid=(batch, nt),
        in_specs=in_specs,
        out_specs=row(d),
        out_shape=jax.ShapeDtypeStruct((m, d), F32),
        scratch_shapes=[pltpu.VMEM((MIX_ROWS, q.shape[1]), BF16),
                        pltpu.VMEM((MIX_ROWS, d), BF16),
                        pltpu.VMEM((MIX_ROWS, d), F32)],
        compiler_params=pltpu.CompilerParams(
            dimension_semantics=("arbitrary", "arbitrary"),
            vmem_limit_bytes=_vmem_limit(weights + tiles + temps),
        ),
        name="mixer",
    )(*args)


def kernel(x, c, w_ada, b_ada, w_in, w_pool_grp, pool_scale, sinks, w_pool_up, w_attn_up,
           w_out, ln_g, ln_b):
    batch, seq, d = x.shape
    depth = w_ada.shape[0]
    pool_width = w_pool_grp.shape[1] * w_pool_grp.shape[2]
    attn_width = w_attn_up.shape[1]
    kv_width = N_KV_HEADS * HEAD_DIM
    widths = (pool_width, pool_width, attn_width, kv_width, kv_width, attn_width, 2 * d)
    assert sum(widths) == w_in.shape[2]
    assert seq % MIX_ROWS == 0 and seq % INPROJ_ROWS == 0 and MIX_ROWS % ATTN_BLOCK == 0
    alpha = (2 * depth) ** 0.25

    mod = _ada_call(c, w_ada, b_ada)
    shift, scale, gate = [mod[:, :, i * d:(i + 1) * d].reshape(depth, batch, 1, d)
                          for i in range(3)]

    stacks = {"in": w_in, "grp": w_pool_grp.reshape(depth, pool_width, -1),
              "pool_up": w_pool_up, "attn_up": w_attn_up, "out": w_out}
    bf16_w = {("in", 0): w_in[0].astype(BF16), ("grp", 0): stacks["grp"][0].astype(BF16)}
    pending = [(name, l) for l in range(depth) for name in stacks if (name, l) not in bf16_w]

    xf = x.reshape(batch * seq, d)
    for l in range(depth):
        convert = [(stacks[name], j) for name, j in pending] if l == 0 else ()
        outs = _inproj_call(xf, scale[l], shift[l], bf16_w["in", l], bf16_w["grp", l],
                            pool_scale[l].reshape(1, -1), widths, seq, convert=convert)
        h_parts = outs[:6]
        if l == 0:
            bf16_w.update(zip(pending, outs[6:]))
        xf = _mix_call(h_parts, xf, gate[l], sinks[l], bf16_w["pool_up", l],
                       bf16_w["attn_up", l], bf16_w["out", l],
                       ln_g[l].reshape(1, d), ln_b[l].reshape(1, d), batch, seq, alpha)
    return xf.reshape(batch, seq, d)
```

```python
import functools

import jax
import jax.numpy as jnp
from jax import lax
from jax.experimental import pallas as pl
from jax.experimental.pallas import tpu as pltpu

F32 = jnp.float32
BF16 = jnp.bfloat16

POOL_WINDOWS = (2, 4, 8, 16)
POOL_GROUP_DIM = 256
MAX_POOL_WINDOW = 16
HEAD_DIM = 64
N_KV_HEADS = 4
Q_PER_KV = 4
ATTN_BLOCK = 128
LN_EPS = 1e-5
LOG2_E = 1.4426950408889634

V7X_VMEM_LIMIT_CAP = 60000 * 1024
BF16_SUBLANES = 16
LANES = 128

ADA_COLS = 1536
INPROJ_ROWS = 256
INPROJ_COLS = 512
MIX_ROWS = 512
MIX_OUT_COLS = 512


def _vmem_limit(estimate_bytes):
    return int(min(V7X_VMEM_LIMIT_CAP, max(32 * 1024 * 1024, estimate_bytes)))


def _ada_kernel(c_ref, w_ref, b_ref, o_ref):
    sc = jax.nn.silu(c_ref[...]).astype(BF16)
    acc = jnp.dot(sc, w_ref[...].astype(BF16), preferred_element_type=F32)
    o_ref[...] = acc + b_ref[...]


def _ada_call(c, w_ada, b_ada):
    depth, d, n = w_ada.shape
    b = c.shape[0]
    return pl.pallas_call(
        _ada_kernel,
        grid=(depth, n // ADA_COLS),
        in_specs=[
            pl.BlockSpec((b, d), lambda l, j: (0, 0)),
            pl.BlockSpec((None, d, ADA_COLS), lambda l, j: (l, 0, j)),
            pl.BlockSpec((None, 1, ADA_COLS), lambda l, j: (l, 0, j)),
        ],
        out_specs=pl.BlockSpec((None, b, ADA_COLS), lambda l, j: (l, 0, j)),
        out_shape=jax.ShapeDtypeStruct((depth, b, n), F32),
        compiler_params=pltpu.CompilerParams(
            dimension_semantics=("arbitrary", "arbitrary"),
            vmem_limit_bytes=_vmem_limit(2 * d * ADA_COLS * 4 + 8 * 1024 * 1024),
        ),
        name="ada_mod",
    )(c, w_ada, b_ada.reshape(depth, 1, n))


def _pool_window(g, p, prev, pos):
    w = POOL_WINDOWS[g]
    s = jnp.concatenate([prev, p], axis=0)
    span = 1
    while span < w:
        s = s + pltpu.roll(s, span, 0)
        span *= 2
    win_sum = s[MAX_POOL_WINDOW:, :]
    count = jnp.minimum(pos + 1, w).astype(F32)
    return win_sum / count - p


def _inproj_kernel(x_ref, scale_ref, shift_ref, w_ref, wgrp_ref, pscale_ref, *rest,
                   widths, n_convert, tiles_per_seq):
    slabs_in, rest = rest[:n_convert], rest[n_convert:]
    ypool_ref, q_ref, k_ref, v_ref, agate_ref, g_ref = rest[:6]
    slabs_out = rest[6:6 + n_convert]
    u_scr, mixed_scr, tail_scr = rest[6 + n_convert:]
    rows = x_ref.shape[0]
    i = pl.program_id(0)
    t = i % tiles_per_seq
    first_tile = t == 0

    @pl.when(i == 0)
    def _():
        tail_scr[...] = jnp.zeros(tail_scr.shape, tail_scr.dtype)

    u_scr[...] = (x_ref[...] * (1.0 + scale_ref[...]) + shift_ref[...]).astype(BF16)

    def project(c0, c1):
        return jnp.dot(u_scr[...], w_ref[:, c0:c1], preferred_element_type=F32)

    pool_width = widths[0]
    pos = t * rows + lax.broadcasted_iota(jnp.int32, (rows, 1), 0)
    groups_per_chunk = INPROJ_COLS // POOL_GROUP_DIM
    for c in range(0, pool_width, INPROJ_COLS):
        acc = project(c, c + INPROJ_COLS)
        for j in range(groups_per_chunk):
            lo = c + j * POOL_GROUP_DIM
            pm = jnp.dot(acc[:, j * POOL_GROUP_DIM:(j + 1) * POOL_GROUP_DIM].astype(BF16),
                         wgrp_ref[lo:lo + POOL_GROUP_DIM, :], preferred_element_type=F32)
            prev = jnp.where(first_tile, 0.0, tail_scr[:, lo:lo + POOL_GROUP_DIM])
            tail_scr[:, lo:lo + POOL_GROUP_DIM] = pm[rows - MAX_POOL_WINDOW:, :]
            mixed_scr[:, lo:lo + POOL_GROUP_DIM] = _pool_window(
                lo // POOL_GROUP_DIM, pm, prev, pos)
    col = pool_width
    for c in range(0, pool_width, INPROJ_COLS):
        acc = project(col + c, col + c + INPROJ_COLS)
        ypool_ref[:, c:c + INPROJ_COLS] = (
            mixed_scr[:, c:c + INPROJ_COLS] * pscale_ref[:, c:c + INPROJ_COLS]
            * jax.nn.silu(acc)).astype(BF16)
    col += pool_width

    epilogues = (
        (q_ref, lambda a: a * (HEAD_DIM ** -0.5 * LOG2_E)),
        (k_ref, None),
        (v_ref, None),
        (agate_ref, jax.nn.silu),
        (g_ref, jax.nn.sigmoid),
    )
    for (o_ref, fn), width in zip(epilogues, widths[2:]):
        step = min(INPROJ_COLS, width)
        for c in range(0, width, step):
            acc = project(col + c, col + c + step)
            if fn is not None:
                acc = fn(acc)
            o_ref[:, c:c + step] = acc.astype(o_ref.dtype)
        col += width

    for src, dst in zip(slabs_in, slabs_out):
        dst[...] = src[...].astype(BF16)


def _inproj_call(x, scale, shift, w_in, wgrp, pscale, widths, seq, convert=()):
    m, d = x.shape
    n_in = w_in.shape[1]
    steps = m // INPROJ_ROWS
    tiles_per_seq = seq // INPROJ_ROWS
    pool_width = widths[0]
    assert widths[1] == pool_width and pool_width % INPROJ_COLS == 0
    row_spec = lambda width: pl.BlockSpec((INPROJ_ROWS, width), lambda i: (i, 0))
    full = lambda a: pl.BlockSpec(a.shape, lambda i: (0,) * a.ndim,
                                  pipeline_mode=pl.Buffered(1))
    mod_spec = pl.BlockSpec((None, 1, d), lambda i: (i // tiles_per_seq, 0, 0))
    out_widths = (pool_width,) + tuple(widths[2:])
    out_shape = [jax.ShapeDtypeStruct((m, width), BF16) for width in out_widths]
    out_specs = [row_spec(width) for width in out_widths]
    vmem = (d * n_in * 2 + 2 * INPROJ_ROWS * d * 4 + 2 * INPROJ_ROWS * sum(out_widths) * 2
            + INPROJ_ROWS * d * 2 + INPROJ_ROWS * pool_width * 4 + 8 * 1024 * 1024)
    slab_in_specs, slab_out_specs = [], []
    for a, layer in convert:
        _, r, c = a.shape
        slab = r // steps
        assert slab * steps == r and slab % BF16_SUBLANES == 0
        slab_in_specs.append(pl.BlockSpec((None, slab, c), lambda i, layer=layer: (layer, i, 0)))
        slab_out_specs.append(pl.BlockSpec((slab, c), lambda i: (i, 0)))
        out_shape.append(jax.ShapeDtypeStruct((r, c), BF16))
        vmem += 2 * slab * c * (4 + 2)
    return pl.pallas_call(
        functools.partial(_inproj_kernel, widths=widths, n_convert=len(convert),
                          tiles_per_seq=tiles_per_seq),
        grid=(steps,),
        in_specs=[row_spec(d), mod_spec, mod_spec, full(w_in), full(wgrp), full(pscale)]
        + slab_in_specs,
        out_specs=out_specs + slab_out_specs,
        out_shape=out_shape,
        scratch_shapes=[pltpu.VMEM((INPROJ_ROWS, d), BF16),
                        pltpu.VMEM((INPROJ_ROWS, pool_width), F32),
                        pltpu.VMEM((MAX_POOL_WINDOW, pool_width), F32)],
        compiler_params=pltpu.CompilerParams(dimension_semantics=("arbitrary",),
                                             vmem_limit_bytes=_vmem_limit(vmem)),
        name="inproj",
    )(x, scale, shift, w_in, wgrp, pscale, *[a for a, _ in convert])


def _attention(q_ref, k_ref, kprev_ref, v_ref, vprev_ref, sinks_ref, first_tile):
    rows = q_ref.shape[0]
    stack = Q_PER_KV * ATTN_BLOCK
    kext = jnp.concatenate([kprev_ref[...], k_ref[...]], axis=0)
    vext = jnp.concatenate([vprev_ref[...], v_ref[...]], axis=0)
    kj = lax.broadcasted_iota(jnp.int32, (2 * ATTN_BLOCK, stack), 0)
    qi = lax.broadcasted_iota(jnp.int32, (2 * ATTN_BLOCK, stack), 1) & (ATTN_BLOCK - 1)
    band = (kj > qi) & (kj <= qi + ATTN_BLOCK)
    bias_inner = jnp.where(band, 0.0, -jnp.inf).astype(F32)
    bias_first = jnp.where(band & ((kj >= ATTN_BLOCK) | jnp.logical_not(first_tile)),
                           0.0, -jnp.inf).astype(F32)
    blocks = []
    for qb in range(rows // ATTN_BLOCK):
        r0 = qb * ATTN_BLOCK
        bias = bias_first if qb == 0 else bias_inner
        pieces = []
        for h in range(N_KV_HEADS):
            qh = q_ref[r0:r0 + ATTN_BLOCK, h * Q_PER_KV * HEAD_DIM:(h + 1) * Q_PER_KV * HEAD_DIM]
            qs = jnp.concatenate(
                [qh[:, g * HEAD_DIM:(g + 1) * HEAD_DIM] for g in range(Q_PER_KV)], axis=0)
            kh = kext[r0:r0 + 2 * ATTN_BLOCK, h * HEAD_DIM:(h + 1) * HEAD_DIM]
            vh = vext[r0:r0 + 2 * ATTN_BLOCK, h * HEAD_DIM:(h + 1) * HEAD_DIM]
            st = lax.dot_general(kh, qs, (((1,), (1,)), ((), ())),
                                 preferred_element_type=F32) + bias
            sink = LOG2_E * jnp.concatenate(
                [jnp.full((1, ATTN_BLOCK), sinks_ref[h * Q_PER_KV + g], F32)
                 for g in range(Q_PER_KV)], axis=1)
            mx = jnp.maximum(jnp.max(st, axis=0, keepdims=True), sink)
            e = jnp.exp2(st - mx)
            denom = jnp.sum(e, axis=0, keepdims=True) + jnp.exp2(sink - mx)
            probs = (e * (1.0 / denom)).astype(BF16)
            o = lax.dot_general(probs, vh, (((0,), (0,)), ((), ())),
                                preferred_element_type=F32)
            pieces.extend(o[g * ATTN_BLOCK:(g + 1) * ATTN_BLOCK] for g in range(Q_PER_KV))
        blocks.append(jnp.concatenate(pieces, axis=1))
    return jnp.concatenate(blocks, axis=0)


def _mix_kernel(sinks_ref, ypool_ref, q_ref, k_ref, kprev_ref, v_ref, vprev_ref, agate_ref,
                g_ref, x_ref, gate_ref, wpu_ref, wau_ref, wout_ref, lng_ref, lnb_ref, xo_ref,
                yattn_scr, merged_scr, cen_scr, *, alpha):
    rows, d = x_ref.shape
    first_tile = pl.program_id(1) == 0

    attn = _attention(q_ref, k_ref, kprev_ref, v_ref, vprev_ref, sinks_ref, first_tile)
    y_attn = (attn * agate_ref[...].astype(F32)).astype(BF16)

    yattn_scr[...] = y_attn
    for lo in range(0, d, MIX_OUT_COLS):
        hi = lo + MIX_OUT_COLS
        up_pool = jnp.dot(ypool_ref[...], wpu_ref[:, lo:hi], preferred_element_type=F32)
        up_attn = jnp.dot(yattn_scr[...], wau_ref[:, lo:hi], preferred_element_type=F32)
        merged_scr[:, lo:hi] = (g_ref[:, lo:hi].astype(F32) * up_pool
                                + g_ref[:, d + lo:d + hi].astype(F32) * up_attn).astype(BF16)

    lanes = LANES
    s1 = jnp.zeros((rows, lanes), F32)
    s2 = jnp.zeros((rows, lanes), F32)
    shift = None
    for lo in range(0, d, MIX_OUT_COLS):
        hi = lo + MIX_OUT_COLS
        out = jnp.dot(merged_scr[...], wout_ref[:, lo:hi], preferred_element_type=F32)
        y = alpha * x_ref[:, lo:hi] + gate_ref[:, lo:hi] * out
        if shift is None:
            shift = jnp.mean(y, axis=-1, keepdims=True)
        cen = y - shift
        cen_scr[:, lo:hi] = cen
        for j in range(0, MIX_OUT_COLS, lanes):
            piece = cen[:, j:j + lanes]
            s1 = s1 + piece
            s2 = s2 + piece * piece
    mu = jnp.sum(s1, axis=-1, keepdims=True) * (1.0 / d)
    var = jnp.sum(s2, axis=-1, keepdims=True) * (1.0 / d) - mu * mu
    rstd = lax.rsqrt(var + LN_EPS)
    xo_ref[...] = (cen_scr[...] - mu) * rstd * lng_ref[...] + lnb_ref[...]


def _mix_call(h_parts, x, gate, sinks, wpu, wau, wout, lng, lnb, batch, seq, alpha):
    ypool, q, k, v, agate, g = h_parts
    m, d = x.shape
    nt = seq // MIX_ROWS

    row = lambda width: pl.BlockSpec((MIX_ROWS, width), lambda b, t: (b * nt + t, 0))

    def prev_spec(width):
        per_tile, per_seq = MIX_ROWS // ATTN_BLOCK, seq // ATTN_BLOCK
        return pl.BlockSpec(
            (ATTN_BLOCK, width),
            lambda b, t: (b * per_seq + jnp.maximum(t * per_tile - 1, 0), 0))

    full = lambda a: pl.BlockSpec(a.shape, lambda b, t: (0,) * a.ndim,
                                  pipeline_mode=pl.Buffered(1))
    mod_spec = pl.BlockSpec((None, 1, d), lambda b, t: (b, 0, 0))

    in_specs = [
        pl.BlockSpec(memory_space=pltpu.SMEM),
        row(ypool.shape[1]), row(q.shape[1]),
        row(k.shape[1]), prev_spec(k.shape[1]),
        row(v.shape[1]), prev_spec(v.shape[1]),
        row(agate.shape[1]), row(g.shape[1]), row(d), mod_spec,
        full(wpu), full(wau), full(wout), full(lng), full(lnb),
    ]
    args = [sinks, ypool, q, k, k, v, v, agate, g, x, gate, wpu, wau, wout, lng, lnb]

    h_cols = sum(a.shape[1] for a in h_parts)
    weights = 2 * (wpu.size + wau.size + wout.size)
    tiles = 2 * MIX_ROWS * (h_cols * 2 + d * 4 + d * 4)
    temps = 10 * MIX_ROWS * d * 4
    return pl.pallas_call(
        functools.partial(_mix_kernel, alpha=alpha),
        grid=(batch, nt),
        in_specs=in_specs,
        out_specs=row(d),
        out_shape=jax.ShapeDtypeStruct((m, d), F32),
        scratch_shapes=[pltpu.VMEM((MIX_ROWS, q.shape[1]), BF16),
                        pltpu.VMEM((MIX_ROWS, d), BF16),
                        pltpu.VMEM((MIX_ROWS, d), F32)],
        compiler_params=pltpu.CompilerParams(
            dimension_semantics=("arbitrary", "arbitrary"),
            vmem_limit_bytes=_vmem_limit(weights + tiles + temps),
        ),
        name="mixer",
    )(*args)


def kernel(x, c, w_ada, b_ada, w_in, w_pool_grp, pool_scale, sinks, w_pool_up, w_attn_up,
           w_out, ln_g, ln_b):
    batch, seq, d = x.shape
    depth = w_ada.shape[0]
    pool_width = w_pool_grp.shape[1] * w_pool_grp.shape[2]
    attn_width = w_attn_up.shape[1]
    kv_width = N_KV_HEADS * HEAD_DIM
    widths = (pool_width, pool_width, attn_width, kv_width, kv_width, attn_width, 2 * d)
    assert sum(widths) == w_in.shape[2]
    assert seq % MIX_ROWS == 0 and seq % INPROJ_ROWS == 0 and MIX_ROWS % ATTN_BLOCK == 0
    alpha = (2 * depth) ** 0.25

    mod = _ada_call(c, w_ada, b_ada)
    shift, scale, gate = [mod[:, :, i * d:(i + 1) * d].reshape(depth, batch, 1, d)
                          for i in range(3)]

    stacks = {"in": w_in, "grp": w_pool_grp.reshape(depth, pool_width, -1),
              "pool_up": w_pool_up, "attn_up": w_attn_up, "out": w_out}
    bf16_w = {("in", 0): w_in[0].astype(BF16), ("grp", 0): stacks["grp"][0].astype(BF16)}
    pending = [(name, l) for l in range(depth) for name in stacks if (name, l) not in bf16_w]

    xf = x.reshape(batch * seq, d)
    for l in range(depth):
        convert = [(stacks[name], j) for name, j in pending] if l == 0 else ()
        outs = _inproj_call(xf, scale[l], shift[l], bf16_w["in", l], bf16_w["grp", l],
                            pool_scale[l].reshape(1, -1), widths, seq, convert=convert)
        h_parts = outs[:6]
        if l == 0:
            bf16_w.update(zip(pending, outs[6:]))
        xf = _mix_call(h_parts, xf, gate[l], sinks[l], bf16_w["pool_up", l],
                       bf16_w["attn_up", l], bf16_w["out", l],
                       ln_g[l].reshape(1, d), ln_b[l].reshape(1, d), batch, seq, alpha)
    return xf.reshape(batch, seq, d)
```

```python
import functools

import jax
import jax.numpy as jnp
from jax import lax
from jax.experimental import pallas as pl
from jax.experimental.pallas import tpu as pltpu

F32 = jnp.float32
BF16 = jnp.bfloat16

POOL_WINDOWS = (2, 4, 8, 16)
POOL_GROUP_DIM = 256
MAX_POOL_WINDOW = 16
HEAD_DIM = 64
N_KV_HEADS = 4
Q_PER_KV = 4
ATTN_BLOCK = 128
LN_EPS = 1e-5
LOG2_E = 1.4426950408889634

V7X_VMEM_LIMIT_CAP = 60000 * 1024
BF16_SUBLANES = 16
LANES = 128

ADA_COLS = 1536
INPROJ_ROWS = 256
INPROJ_COLS = 1024
MIX_ROWS = 512
MIX_OUT_COLS = 512


def _vmem_limit(estimate_bytes):
    return int(min(V7X_VMEM_LIMIT_CAP, max(32 * 1024 * 1024, estimate_bytes)))


def _ada_kernel(c_ref, w_ref, b_ref, o_ref):
    sc = jax.nn.silu(c_ref[...]).astype(BF16)
    acc = jnp.dot(sc, w_ref[...].astype(BF16), preferred_element_type=F32)
    o_ref[...] = acc + b_ref[...]


def _ada_call(c, w_ada, b_ada):
    depth, d, n = w_ada.shape
    b = c.shape[0]
    return pl.pallas_call(
        _ada_kernel,
        grid=(depth, n // ADA_COLS),
        in_specs=[
            pl.BlockSpec((b, d), lambda l, j: (0, 0)),
            pl.BlockSpec((None, d, ADA_COLS), lambda l, j: (l, 0, j)),
            pl.BlockSpec((None, 1, ADA_COLS), lambda l, j: (l, 0, j)),
        ],
        out_specs=pl.BlockSpec((None, b, ADA_COLS), lambda l, j: (l, 0, j)),
        out_shape=jax.ShapeDtypeStruct((depth, b, n), F32),
        compiler_params=pltpu.CompilerParams(
            dimension_semantics=("arbitrary", "arbitrary"),
            vmem_limit_bytes=_vmem_limit(2 * d * ADA_COLS * 4 + 8 * 1024 * 1024),
        ),
        name="ada_mod",
    )(c, w_ada, b_ada.reshape(depth, 1, n))


def _pool_window(g, p, prev, pos):
    w = POOL_WINDOWS[g]
    s = jnp.concatenate([prev, p], axis=0)
    span = 1
    while span < w:
        s = s + pltpu.roll(s, span, 0)
        span *= 2
    win_sum = s[MAX_POOL_WINDOW:, :]
    count = jnp.minimum(pos + 1, w).astype(F32)
    return win_sum / count - p


def _inproj_kernel(x_ref, scale_ref, shift_ref, w_ref, wgrp_ref, pscale_ref, *rest,
                   widths, n_convert, tiles_per_seq):
    slabs_in, rest = rest[:n_convert], rest[n_convert:]
    ypool_ref, q_ref, k_ref, v_ref, agate_ref, g_ref = rest[:6]
    slabs_out = rest[6:6 + n_convert]
    u_scr, mixed_scr, tail_scr = rest[6 + n_convert:]
    rows = x_ref.shape[0]
    i = pl.program_id(0)
    t = i % tiles_per_seq
    first_tile = t == 0

    @pl.when(i == 0)
    def _():
        tail_scr[...] = jnp.zeros(tail_scr.shape, tail_scr.dtype)

    u_scr[...] = (x_ref[...] * (1.0 + scale_ref[...]) + shift_ref[...]).astype(BF16)

    def project(c0, c1):
        return jnp.dot(u_scr[...], w_ref[:, c0:c1], preferred_element_type=F32)

    pool_width = widths[0]
    pos = t * rows + lax.broadcasted_iota(jnp.int32, (rows, 1), 0)
    pool_step = min(INPROJ_COLS, pool_width)
    for c in range(0, pool_width, pool_step):
        acc = project(c, c + pool_step)
        for j in range(pool_step // POOL_GROUP_DIM):
            lo = c + j * POOL_GROUP_DIM
            pm = jnp.dot(acc[:, j * POOL_GROUP_DIM:(j + 1) * POOL_GROUP_DIM].astype(BF16),
                         wgrp_ref[lo:lo + POOL_GROUP_DIM, :], preferred_element_type=F32)
            prev = jnp.where(first_tile, 0.0, tail_scr[:, lo:lo + POOL_GROUP_DIM])
            tail_scr[:, lo:lo + POOL_GROUP_DIM] = pm[rows - MAX_POOL_WINDOW:, :]
            mixed_scr[:, lo:lo + POOL_GROUP_DIM] = _pool_window(
                lo // POOL_GROUP_DIM, pm, prev, pos)
    col = pool_width
    for c in range(0, pool_width, pool_step):
        acc = project(col + c, col + c + pool_step)
        ypool_ref[:, c:c + pool_step] = (
            mixed_scr[:, c:c + pool_step] * pscale_ref[:, c:c + pool_step]
            * jax.nn.silu(acc)).astype(BF16)
    col += pool_width

    epilogues = (
        (q_ref, lambda a: a * (HEAD_DIM ** -0.5 * LOG2_E)),
        (k_ref, None),
        (v_ref, None),
        (agate_ref, jax.nn.silu),
        (g_ref, jax.nn.sigmoid),
    )
    for (o_ref, fn), width in zip(epilogues, widths[2:]):
        step = min(INPROJ_COLS, width)
        for c in range(0, width, step):
            acc = project(col + c, col + c + step)
            if fn is not None:
                acc = fn(acc)
            o_ref[:, c:c + step] = acc.astype(o_ref.dtype)
        col += width

    for src, dst in zip(slabs_in, slabs_out):
        dst[...] = src[...].astype(BF16)


def _inproj_call(x, scale, shift, w_in, wgrp, pscale, widths, seq, convert=()):
    m, d = x.shape
    n_in = w_in.shape[1]
    steps = m // INPROJ_ROWS
    tiles_per_seq = seq // INPROJ_ROWS
    pool_width = widths[0]
    assert widths[1] == pool_width and pool_width % min(INPROJ_COLS, pool_width) == 0
    row_spec = lambda width: pl.BlockSpec((INPROJ_ROWS, width), lambda i: (i, 0))
    full = lambda a: pl.BlockSpec(a.shape, lambda i: (0,) * a.ndim,
                                  pipeline_mode=pl.Buffered(1))
    mod_spec = pl.BlockSpec((None, 1, d), lambda i: (i // tiles_per_seq, 0, 0))
    out_widths = (pool_width,) + tuple(widths[2:])
    out_shape = [jax.ShapeDtypeStruct((m, width), BF16) for width in out_widths]
    out_specs = [row_spec(width) for width in out_widths]
    vmem = (d * n_in * 2 + 2 * INPROJ_ROWS * d * 4 + 2 * INPROJ_ROWS * sum(out_widths) * 2
            + INPROJ_ROWS * d * 2 + INPROJ_ROWS * pool_width * 4 + 8 * 1024 * 1024)
    slab_in_specs, slab_out_specs = [], []
    for a, layer in convert:
        _, r, c = a.shape
        slab = r // steps
        assert slab * steps == r and slab % BF16_SUBLANES == 0
        slab_in_specs.append(pl.BlockSpec((None, slab, c), lambda i, layer=layer: (layer, i, 0)))
        slab_out_specs.append(pl.BlockSpec((slab, c), lambda i: (i, 0)))
        out_shape.append(jax.ShapeDtypeStruct((r, c), BF16))
        vmem += 2 * slab * c * (4 + 2)
    return pl.pallas_call(
        functools.partial(_inproj_kernel, widths=widths, n_convert=len(convert),
                          tiles_per_seq=tiles_per_seq),
        grid=(steps,),
        in_specs=[row_spec(d), mod_spec, mod_spec, full(w_in), full(wgrp), full(pscale)]
        + slab_in_specs,
        out_specs=out_specs + slab_out_specs,
        out_shape=out_shape,
        scratch_shapes=[pltpu.VMEM((INPROJ_ROWS, d), BF16),
                        pltpu.VMEM((INPROJ_ROWS, pool_width), F32),
                        pltpu.VMEM((MAX_POOL_WINDOW, pool_width), F32)],
        compiler_params=pltpu.CompilerParams(dimension_semantics=("arbitrary",),
                                             vmem_limit_bytes=_vmem_limit(vmem)),
        name="inproj",
    )(x, scale, shift, w_in, wgrp, pscale, *[a for a, _ in convert])


def _attention(q_ref, k_ref, kprev_ref, v_ref, vprev_ref, sinks_ref, first_tile):
    rows = q_ref.shape[0]
    stack = Q_PER_KV * ATTN_BLOCK
    kext = jnp.concatenate([kprev_ref[...], k_ref[...]], axis=0)
    vext = jnp.concatenate([vprev_ref[...], v_ref[...]], axis=0)
    kj = lax.broadcasted_iota(jnp.int32, (2 * ATTN_BLOCK, stack), 0)
    qi = lax.broadcasted_iota(jnp.int32, (2 * ATTN_BLOCK, stack), 1) & (ATTN_BLOCK - 1)
    band = (kj > qi) & (kj <= qi + ATTN_BLOCK)
    bias_inner = jnp.where(band, 0.0, -jnp.inf).astype(F32)
    bias_first = jnp.where(band & ((kj >= ATTN_BLOCK) | jnp.logical_not(first_tile)),
                           0.0, -jnp.inf).astype(F32)
    blocks = []
    for qb in range(rows // ATTN_BLOCK):
        r0 = qb * ATTN_BLOCK
        bias = bias_first if qb == 0 else bias_inner
        pieces = []
        for h in range(N_KV_HEADS):
            qh = q_ref[r0:r0 + ATTN_BLOCK, h * Q_PER_KV * HEAD_DIM:(h + 1) * Q_PER_KV * HEAD_DIM]
            qs = jnp.concatenate(
                [qh[:, g * HEAD_DIM:(g + 1) * HEAD_DIM] for g in range(Q_PER_KV)], axis=0)
            kh = kext[r0:r0 + 2 * ATTN_BLOCK, h * HEAD_DIM:(h + 1) * HEAD_DIM]
            vh = vext[r0:r0 + 2 * ATTN_BLOCK, h * HEAD_DIM:(h + 1) * HEAD_DIM]
            st = lax.dot_general(kh, qs, (((1,), (1,)), ((), ())),
                                 preferred_element_type=F32) + bias
            sink = LOG2_E * jnp.concatenate(
                [jnp.full((1, ATTN_BLOCK), sinks_ref[h * Q_PER_KV + g], F32)
                 for g in range(Q_PER_KV)], axis=1)
            mx = jnp.maximum(jnp.max(st, axis=0, keepdims=True), sink)
            e = jnp.exp2(st - mx)
            denom = jnp.sum(e, axis=0, keepdims=True) + jnp.exp2(sink - mx)
            probs = (e * (1.0 / denom)).astype(BF16)
            o = lax.dot_general(probs, vh, (((0,), (0,)), ((), ())),
                                preferred_element_type=F32)
            pieces.extend(o[g * ATTN_BLOCK:(g + 1) * ATTN_BLOCK] for g in range(Q_PER_KV))
        blocks.append(jnp.concatenate(pieces, axis=1))
    return jnp.concatenate(blocks, axis=0)


def _mix_kernel(sinks_ref, ypool_ref, q_ref, k_ref, kprev_ref, v_ref, vprev_ref, agate_ref,
                g_ref, x_ref, gate_ref, wpu_ref, wau_ref, wout_ref, lng_ref, lnb_ref, xo_ref,
                yattn_scr, merged_scr, cen_scr, *, alpha):
    rows, d = x_ref.shape
    first_tile = pl.program_id(1) == 0

    attn = _attention(q_ref, k_ref, kprev_ref, v_ref, vprev_ref, sinks_ref, first_tile)
    y_attn = (attn * agate_ref[...].astype(F32)).astype(BF16)

    yattn_scr[...] = y_attn
    for lo in range(0, d, MIX_OUT_COLS):
        hi = lo + MIX_OUT_COLS
        up_pool = jnp.dot(ypool_ref[...], wpu_ref[:, lo:hi], preferred_element_type=F32)
        up_attn = jnp.dot(yattn_scr[...], wau_ref[:, lo:hi], preferred_element_type=F32)
        merged_scr[:, lo:hi] = (g_ref[:, lo:hi].astype(F32) * up_pool
                                + g_ref[:, d + lo:d + hi].astype(F32) * up_attn).astype(BF16)

    lanes = LANES
    s1 = jnp.zeros((rows, lanes), F32)
    s2 = jnp.zeros((rows, lanes), F32)
    shift = None
    for lo in range(0, d, MIX_OUT_COLS):
        hi = lo + MIX_OUT_COLS
        out = jnp.dot(merged_scr[...], wout_ref[:, lo:hi], preferred_element_type=F32)
        y = alpha * x_ref[:, lo:hi] + gate_ref[:, lo:hi] * out
        if shift is None:
            shift = jnp.mean(y, axis=-1, keepdims=True)
        cen = y - shift
        cen_scr[:, lo:hi] = cen
        for j in range(0, MIX_OUT_COLS, lanes):
            piece = cen[:, j:j + lanes]
            s1 = s1 + piece
            s2 = s2 + piece * piece
    mu = jnp.sum(s1, axis=-1, keepdims=True) * (1.0 / d)
    var = jnp.sum(s2, axis=-1, keepdims=True) * (1.0 / d) - mu * mu
    rstd = lax.rsqrt(var + LN_EPS)
    xo_ref[...] = (cen_scr[...] - mu) * rstd * lng_ref[...] + lnb_ref[...]


def _mix_call(h_parts, x, gate, sinks, wpu, wau, wout, lng, lnb, batch, seq, alpha):
    ypool, q, k, v, agate, g = h_parts
    m, d = x.shape
    nt = seq // MIX_ROWS

    row = lambda width: pl.BlockSpec((MIX_ROWS, width), lambda b, t: (b * nt + t, 0))

    def prev_spec(width):
        per_tile, per_seq = MIX_ROWS // ATTN_BLOCK, seq // ATTN_BLOCK
        return pl.BlockSpec(
            (ATTN_BLOCK, width),
            lambda b, t: (b * per_seq + jnp.maximum(t * per_tile - 1, 0), 0))

    full = lambda a: pl.BlockSpec(a.shape, lambda b, t: (0,) * a.ndim,
                                  pipeline_mode=pl.Buffered(1))
    mod_spec = pl.BlockSpec((None, 1, d), lambda b, t: (b, 0, 0))

    in_specs = [
        pl.BlockSpec(memory_space=pltpu.SMEM),
        row(ypool.shape[1]), row(q.shape[1]),
        row(k.shape[1]), prev_spec(k.shape[1]),
        row(v.shape[1]), prev_spec(v.shape[1]),
        row(agate.shape[1]), row(g.shape[1]), row(d), mod_spec,
        full(wpu), full(wau), full(wout), full(lng), full(lnb),
    ]
    args = [sinks, ypool, q, k, k, v, v, agate, g, x, gate, wpu, wau, wout, lng, lnb]

    h_cols = sum(a.shape[1] for a in h_parts)
    weights = 2 * (wpu.size + wau.size + wout.size)
    tiles = 2 * MIX_ROWS * (h_cols * 2 + d * 4 + d * 4)
    temps = 10 * MIX_ROWS * d * 4
    return pl.pallas_call(
        functools.partial(_mix_kernel, alpha=alpha),
        grid=(batch, nt),
        in_specs=in_specs,
        out_specs=row(d),
        out_shape=jax.ShapeDtypeStruct((m, d), F32),
        scratch_shapes=[pltpu.VMEM((MIX_ROWS, q.shape[1]), BF16),
                        pltpu.VMEM((MIX_ROWS, d), BF16),
                        pltpu.VMEM((MIX_ROWS, d), F32)],
        compiler_params=pltpu.CompilerParams(
            dimension_semantics=("arbitrary", "arbitrary"),
            vmem_limit_bytes=_vmem_limit(weights + tiles + temps),
        ),
        name="mixer",
    )(*args)


def kernel(x, c, w_ada, b_ada, w_in, w_pool_grp, pool_scale, sinks, w_pool_up, w_attn_up,
           w_out, ln_g, ln_b):
    batch, seq, d = x.shape
    depth = w_ada.shape[0]
    pool_width = w_pool_grp.shape[1] * w_pool_grp.shape[2]
    attn_width = w_attn_up.shape[1]
    kv_width = N_KV_HEADS * HEAD_DIM
    widths = (pool_width, pool_width, attn_width, kv_width, kv_width, attn_width, 2 * d)
    assert sum(widths) == w_in.shape[2]
    assert seq % MIX_ROWS == 0 and seq % INPROJ_ROWS == 0 and MIX_ROWS % ATTN_BLOCK == 0
    alpha = (2 * depth) ** 0.25

    mod = _ada_call(c, w_ada, b_ada)
    shift, scale, gate = [mod[:, :, i * d:(i + 1) * d].reshape(depth, batch, 1, d)
                          for i in range(3)]

    stacks = {"in": w_in, "grp": w_pool_grp.reshape(depth, pool_width, -1),
              "pool_up": w_pool_up, "attn_up": w_attn_up, "out": w_out}
    bf16_w = {("in", 0): w_in[0].astype(BF16), ("grp", 0): stacks["grp"][0].astype(BF16)}
    pending = [(name, l) for l in range(depth) for name in stacks if (name, l) not in bf16_w]

    xf = x.reshape(batch * seq, d)
    for l in range(depth):
        convert = [(stacks[name], j) for name, j in pending] if l == 0 else ()
        outs = _inproj_call(xf, scale[l], shift[l], bf16_w["in", l], bf16_w["grp", l],
                            pool_scale[l].reshape(1, -1), widths, seq, convert=convert)
        h_parts = outs[:6]
        if l == 0:
            bf16_w.update(zip(pending, outs[6:]))
        xf = _mix_call(h_parts, xf, gate[l], sinks[l], bf16_w["pool_up", l],
                       bf16_w["attn_up", l], bf16_w["out", l],
                       ln_g[l].reshape(1, d), ln_b[l].reshape(1, d), batch, seq, alpha)
    return xf.reshape(batch, seq, d)
```

```python
import functools

import jax
import jax.numpy as jnp
from jax import lax
from jax.experimental import pallas as pl
from jax.experimental.pallas import tpu as pltpu

F32 = jnp.float32
BF16 = jnp.bfloat16

POOL_WINDOWS = (2, 4, 8, 16)
POOL_GROUP_DIM = 256
MAX_POOL_WINDOW = 16
HEAD_DIM = 64
N_KV_HEADS = 4
Q_PER_KV = 4
ATTN_BLOCK = 128
LN_EPS = 1e-5
LOG2_E = 1.4426950408889634

V7X_VMEM_LIMIT_CAP = 60000 * 1024
BF16_SUBLANES = 16
LANES = 128

ADA_COLS = 1536
INPROJ_ROWS = 256
INPROJ_COLS = 1024
MIX_ROWS = 512
MIX_OUT_COLS = 512


def _vmem_limit(estimate_bytes):
    return int(min(V7X_VMEM_LIMIT_CAP, max(32 * 1024 * 1024, estimate_bytes)))


def _ada_kernel(c_ref, w_ref, b_ref, o_ref):
    sc = jax.nn.silu(c_ref[...]).astype(BF16)
    acc = jnp.dot(sc, w_ref[...].astype(BF16), preferred_element_type=F32)
    o_ref[...] = acc + b_ref[...]


def _ada_call(c, w_ada, b_ada):
    depth, d, n = w_ada.shape
    b = c.shape[0]
    return pl.pallas_call(
        _ada_kernel,
        grid=(depth, n // ADA_COLS),
        in_specs=[
            pl.BlockSpec((b, d), lambda l, j: (0, 0)),
            pl.BlockSpec((None, d, ADA_COLS), lambda l, j: (l, 0, j)),
            pl.BlockSpec((None, 1, ADA_COLS), lambda l, j: (l, 0, j)),
        ],
        out_specs=pl.BlockSpec((None, b, ADA_COLS), lambda l, j: (l, 0, j)),
        out_shape=jax.ShapeDtypeStruct((depth, b, n), F32),
        compiler_params=pltpu.CompilerParams(
            dimension_semantics=("arbitrary", "arbitrary"),
            vmem_limit_bytes=_vmem_limit(2 * d * ADA_COLS * 4 + 8 * 1024 * 1024),
        ),
        name="ada_mod",
    )(c, w_ada, b_ada.reshape(depth, 1, n))


def _pool_window(g, p, prev, pos):
    w = POOL_WINDOWS[g]
    s = jnp.concatenate([prev, p], axis=0)
    span = 1
    while span < w:
        s = s + pltpu.roll(s, span, 0)
        span *= 2
    win_sum = s[MAX_POOL_WINDOW:, :]
    count = jnp.minimum(pos + 1, w).astype(F32)
    return win_sum / count - p


def _inproj_kernel(x_ref, scale_ref, shift_ref, w_ref, wgrp_ref, pscale_ref, *rest,
                   widths, n_convert, tiles_per_seq, norm_input):
    if norm_input:
        mu_ref, rstd_ref, lng_ref, lnb_ref = rest[:4]
        rest = rest[4:]
    slabs_in, rest = rest[:n_convert], rest[n_convert:]
    if norm_input:
        xo_ref, rest = rest[0], rest[1:]
    ypool_ref, q_ref, k_ref, v_ref, agate_ref, g_ref = rest[:6]
    slabs_out = rest[6:6 + n_convert]
    u_scr, mixed_scr, tail_scr = rest[6 + n_convert:]
    rows = x_ref.shape[0]
    i = pl.program_id(0)
    t = i % tiles_per_seq
    first_tile = t == 0

    @pl.when(i == 0)
    def _():
        tail_scr[...] = jnp.zeros(tail_scr.shape, tail_scr.dtype)

    if norm_input:
        reps = x_ref.shape[1] // LANES
        mu = jnp.concatenate([mu_ref[...]] * reps, axis=1)
        rstd = jnp.concatenate([rstd_ref[...]] * reps, axis=1)
        x = (x_ref[...] - mu) * rstd * lng_ref[...] + lnb_ref[...]
        xo_ref[...] = x
    else:
        x = x_ref[...]
    u_scr[...] = (x * (1.0 + scale_ref[...]) + shift_ref[...]).astype(BF16)

    def project(c0, c1):
        return jnp.dot(u_scr[...], w_ref[:, c0:c1], preferred_element_type=F32)

    pool_width = widths[0]
    pos = t * rows + lax.broadcasted_iota(jnp.int32, (rows, 1), 0)
    pool_step = min(INPROJ_COLS, pool_width)
    for c in range(0, pool_width, pool_step):
        acc = project(c, c + pool_step)
        for j in range(pool_step // POOL_GROUP_DIM):
            lo = c + j * POOL_GROUP_DIM
            pm = jnp.dot(acc[:, j * POOL_GROUP_DIM:(j + 1) * POOL_GROUP_DIM].astype(BF16),
                         wgrp_ref[lo:lo + POOL_GROUP_DIM, :], preferred_element_type=F32)
            prev = jnp.where(first_tile, 0.0, tail_scr[:, lo:lo + POOL_GROUP_DIM])
            tail_scr[:, lo:lo + POOL_GROUP_DIM] = pm[rows - MAX_POOL_WINDOW:, :]
            mixed_scr[:, lo:lo + POOL_GROUP_DIM] = _pool_window(
                lo // POOL_GROUP_DIM, pm, prev, pos)
    col = pool_width
    for c in range(0, pool_width, pool_step):
        acc = project(col + c, col + c + pool_step)
        ypool_ref[:, c:c + pool_step] = (
            mixed_scr[:, c:c + pool_step] * pscale_ref[:, c:c + pool_step]
            * jax.nn.silu(acc)).astype(BF16)
    col += pool_width

    epilogues = (
        (q_ref, lambda a: a * (HEAD_DIM ** -0.5 * LOG2_E)),
        (k_ref, None),
        (v_ref, None),
        (agate_ref, jax.nn.silu),
        (g_ref, jax.nn.sigmoid),
    )
    for (o_ref, fn), width in zip(epilogues, widths[2:]):
        step = min(INPROJ_COLS, width)
        for c in range(0, width, step):
            acc = project(col + c, col + c + step)
            if fn is not None:
                acc = fn(acc)
            o_ref[:, c:c + step] = acc.astype(o_ref.dtype)
        col += width

    for src, dst in zip(slabs_in, slabs_out):
        dst[...] = src[...].astype(BF16)


def _inproj_call(x, scale, shift, w_in, wgrp, pscale, widths, seq, convert=(), norm=None):
    m, d = x.shape
    n_in = w_in.shape[1]
    steps = m // INPROJ_ROWS
    tiles_per_seq = seq // INPROJ_ROWS
    pool_width = widths[0]
    assert widths[1] == pool_width and pool_width % min(INPROJ_COLS, pool_width) == 0
    row_spec = lambda width: pl.BlockSpec((INPROJ_ROWS, width), lambda i: (i, 0))
    full = lambda a: pl.BlockSpec(a.shape, lambda i: (0,) * a.ndim,
                                  pipeline_mode=pl.Buffered(1))
    mod_spec = pl.BlockSpec((None, 1, d), lambda i: (i // tiles_per_seq, 0, 0))
    out_widths = (pool_width,) + tuple(widths[2:])
    out_shape = [jax.ShapeDtypeStruct((m, width), BF16) for width in out_widths]
    out_specs = [row_spec(width) for width in out_widths]
    vmem = (d * n_in * 2 + 2 * INPROJ_ROWS * d * 4 + 2 * INPROJ_ROWS * sum(out_widths) * 2
            + INPROJ_ROWS * d * 2 + INPROJ_ROWS * pool_width * 4 + 8 * 1024 * 1024)
    slab_in_specs, slab_out_specs = [], []
    for a, layer in convert:
        _, r, c = a.shape
        slab = r // steps
        assert slab * steps == r and slab % BF16_SUBLANES == 0
        slab_in_specs.append(pl.BlockSpec((None, slab, c), lambda i, layer=layer: (layer, i, 0)))
        slab_out_specs.append(pl.BlockSpec((slab, c), lambda i: (i, 0)))
        out_shape.append(jax.ShapeDtypeStruct((r, c), BF16))
        vmem += 2 * slab * c * (4 + 2)
    norm_specs, norm_args = [], []
    if norm is not None:
        mean, rstd, gain, bias = norm
        norm_specs = [row_spec(LANES), row_spec(LANES), full(gain), full(bias)]
        norm_args = [mean, rstd, gain, bias]
        out_shape.insert(0, jax.ShapeDtypeStruct((m, d), F32))
        out_specs.insert(0, row_spec(d))
        vmem += 2 * INPROJ_ROWS * d * 4
    return pl.pallas_call(
        functools.partial(_inproj_kernel, widths=widths, n_convert=len(convert),
                          tiles_per_seq=tiles_per_seq, norm_input=norm is not None),
        grid=(steps,),
        in_specs=[row_spec(d), mod_spec, mod_spec, full(w_in), full(wgrp), full(pscale)]
        + norm_specs + slab_in_specs,
        out_specs=out_specs + slab_out_specs,
        out_shape=out_shape,
        scratch_shapes=[pltpu.VMEM((INPROJ_ROWS, d), BF16),
                        pltpu.VMEM((INPROJ_ROWS, pool_width), F32),
                        pltpu.VMEM((MAX_POOL_WINDOW, pool_width), F32)],
        compiler_params=pltpu.CompilerParams(dimension_semantics=("arbitrary",),
                                             vmem_limit_bytes=_vmem_limit(vmem)),
        name="inproj",
    )(x, scale, shift, w_in, wgrp, pscale, *norm_args, *[a for a, _ in convert])


def _attention(q_ref, k_ref, kprev_ref, v_ref, vprev_ref, sinks_ref, first_tile):
    rows = q_ref.shape[0]
    stack = Q_PER_KV * ATTN_BLOCK
    kext = jnp.concatenate([kprev_ref[...], k_ref[...]], axis=0)
    vext = jnp.concatenate([vprev_ref[...], v_ref[...]], axis=0)
    kj = lax.broadcasted_iota(jnp.int32, (2 * ATTN_BLOCK, stack), 0)
    qi = lax.broadcasted_iota(jnp.int32, (2 * ATTN_BLOCK, stack), 1) & (ATTN_BLOCK - 1)
    band = (kj > qi) & (kj <= qi + ATTN_BLOCK)
    bias_inner = jnp.where(band, 0.0, -jnp.inf).astype(F32)
    bias_first = jnp.where(band & ((kj >= ATTN_BLOCK) | jnp.logical_not(first_tile)),
                           0.0, -jnp.inf).astype(F32)
    blocks = []
    for qb in range(rows // ATTN_BLOCK):
        r0 = qb * ATTN_BLOCK
        bias = bias_first if qb == 0 else bias_inner
        pieces = []
        for h in range(N_KV_HEADS):
            qh = q_ref[r0:r0 + ATTN_BLOCK, h * Q_PER_KV * HEAD_DIM:(h + 1) * Q_PER_KV * HEAD_DIM]
            qs = jnp.concatenate(
                [qh[:, g * HEAD_DIM:(g + 1) * HEAD_DIM] for g in range(Q_PER_KV)], axis=0)
            kh = kext[r0:r0 + 2 * ATTN_BLOCK, h * HEAD_DIM:(h + 1) * HEAD_DIM]
            vh = vext[r0:r0 + 2 * ATTN_BLOCK, h * HEAD_DIM:(h + 1) * HEAD_DIM]
            st = lax.dot_general(kh, qs, (((1,), (1,)), ((), ())),
                                 preferred_element_type=F32) + bias
            sink = LOG2_E * jnp.concatenate(
                [jnp.full((1, ATTN_BLOCK), sinks_ref[h * Q_PER_KV + g], F32)
                 for g in range(Q_PER_KV)], axis=1)
            mx = jnp.maximum(jnp.max(st, axis=0, keepdims=True), sink)
            e = jnp.exp2(st - mx)
            denom = jnp.sum(e, axis=0, keepdims=True) + jnp.exp2(sink - mx)
            probs = (e * (1.0 / denom)).astype(BF16)
            o = lax.dot_general(probs, vh, (((0,), (0,)), ((), ())),
                                preferred_element_type=F32)
            pieces.extend(o[g * ATTN_BLOCK:(g + 1) * ATTN_BLOCK] for g in range(Q_PER_KV))
        blocks.append(jnp.concatenate(pieces, axis=1))
    return jnp.concatenate(blocks, axis=0)


def _mix_kernel(sinks_ref, ypool_ref, q_ref, k_ref, kprev_ref, v_ref, vprev_ref, agate_ref,
                g_ref, x_ref, gate_ref, wpu_ref, wau_ref, wout_ref, *rest, alpha, apply_norm):
    if apply_norm:
        lng_ref, lnb_ref, xo_ref, yattn_scr, merged_scr, cen_ref = rest
    else:
        cen_ref, mu_ref, rstd_ref, yattn_scr, merged_scr = rest
    rows, d = x_ref.shape
    first_tile = pl.program_id(1) == 0

    attn = _attention(q_ref, k_ref, kprev_ref, v_ref, vprev_ref, sinks_ref, first_tile)
    y_attn = (attn * agate_ref[...].astype(F32)).astype(BF16)

    yattn_scr[...] = y_attn
    for lo in range(0, d, MIX_OUT_COLS):
        hi = lo + MIX_OUT_COLS
        up_pool = jnp.dot(ypool_ref[...], wpu_ref[:, lo:hi], preferred_element_type=F32)
        up_attn = jnp.dot(yattn_scr[...], wau_ref[:, lo:hi], preferred_element_type=F32)
        merged_scr[:, lo:hi] = (g_ref[:, lo:hi].astype(F32) * up_pool
                                + g_ref[:, d + lo:d + hi].astype(F32) * up_attn).astype(BF16)

    lanes = LANES
    s1 = jnp.zeros((rows, lanes), F32)
    s2 = jnp.zeros((rows, lanes), F32)
    shift = None
    for lo in range(0, d, MIX_OUT_COLS):
        hi = lo + MIX_OUT_COLS
        out = jnp.dot(merged_scr[...], wout_ref[:, lo:hi], preferred_element_type=F32)
        y = alpha * x_ref[:, lo:hi] + gate_ref[:, lo:hi] * out
        if shift is None:
            shift = jnp.mean(y, axis=-1, keepdims=True)
        cen = y - shift
        cen_ref[:, lo:hi] = cen
        for j in range(0, MIX_OUT_COLS, lanes):
            piece = cen[:, j:j + lanes]
            s1 = s1 + piece
            s2 = s2 + piece * piece
    mu = jnp.sum(s1, axis=-1, keepdims=True) * (1.0 / d)
    var = jnp.sum(s2, axis=-1, keepdims=True) * (1.0 / d) - mu * mu
    rstd = lax.rsqrt(var + LN_EPS)
    if apply_norm:
        xo_ref[...] = (cen_ref[...] - mu) * rstd * lng_ref[...] + lnb_ref[...]
    else:
        mu_ref[...] = jnp.broadcast_to(mu, mu_ref.shape)
        rstd_ref[...] = jnp.broadcast_to(rstd, rstd_ref.shape)


def _mix_call(h_parts, x, gate, sinks, wpu, wau, wout, ln, batch, seq, alpha):
    ypool, q, k, v, agate, g = h_parts
    m, d = x.shape
    nt = seq // MIX_ROWS
    apply_norm = ln is not None

    row = lambda width: pl.BlockSpec((MIX_ROWS, width), lambda b, t: (b * nt + t, 0))

    def prev_spec(width):
        per_tile, per_seq = MIX_ROWS // ATTN_BLOCK, seq // ATTN_BLOCK
        return pl.BlockSpec(
            (ATTN_BLOCK, width),
            lambda b, t: (b * per_seq + jnp.maximum(t * per_tile - 1, 0), 0))

    full = lambda a: pl.BlockSpec(a.shape, lambda b, t: (0,) * a.ndim,
                                  pipeline_mode=pl.Buffered(1))
    mod_spec = pl.BlockSpec((None, 1, d), lambda b, t: (b, 0, 0))

    in_specs = [
        pl.BlockSpec(memory_space=pltpu.SMEM),
        row(ypool.shape[1]), row(q.shape[1]),
        row(k.shape[1]), prev_spec(k.shape[1]),
        row(v.shape[1]), prev_spec(v.shape[1]),
        row(agate.shape[1]), row(g.shape[1]), row(d), mod_spec,
        full(wpu), full(wau), full(wout),
    ]
    args = [sinks, ypool, q, k, k, v, v, agate, g, x, gate, wpu, wau, wout]
    scratch_shapes = [pltpu.VMEM((MIX_ROWS, q.shape[1]), BF16),
                      pltpu.VMEM((MIX_ROWS, d), BF16)]
    if apply_norm:
        in_specs += [full(ln[0]), full(ln[1])]
        args += list(ln)
        out_specs = row(d)
        out_shape = jax.ShapeDtypeStruct((m, d), F32)
        scratch_shapes.append(pltpu.VMEM((MIX_ROWS, d), F32))
    else:
        out_specs = [row(d), row(LANES), row(LANES)]
        out_shape = [jax.ShapeDtypeStruct((m, d), F32), jax.ShapeDtypeStruct((m, LANES), F32),
                     jax.ShapeDtypeStruct((m, LANES), F32)]

    h_cols = sum(a.shape[1] for a in h_parts)
    weights = 2 * (wpu.size + wau.size + wout.size)
    tiles = 2 * MIX_ROWS * (h_cols * 2 + d * 4 + d * 4)
    temps = 10 * MIX_ROWS * d * 4
    return pl.pallas_call(
        functools.partial(_mix_kernel, alpha=alpha, apply_norm=apply_norm),
        grid=(batch, nt),
        in_specs=in_specs,
        out_specs=out_specs,
        out_shape=out_shape,
        scratch_shapes=scratch_shapes,
        compiler_params=pltpu.CompilerParams(
            dimension_semantics=("arbitrary", "arbitrary"),
            vmem_limit_bytes=_vmem_limit(weights + tiles + temps),
        ),
        name="mixer",
    )(*args)


def kernel(x, c, w_ada, b_ada, w_in, w_pool_grp, pool_scale, sinks, w_pool_up, w_attn_up,
           w_out, ln_g, ln_b):
    batch, seq, d = x.shape
    depth = w_ada.shape[0]
    pool_width = w_pool_grp.shape[1] * w_pool_grp.shape[2]
    attn_width = w_attn_up.shape[1]
    kv_width = N_KV_HEADS * HEAD_DIM
    widths = (pool_width, pool_width, attn_width, kv_width, kv_width, attn_width, 2 * d)
    assert sum(widths) == w_in.shape[2]
    assert seq % MIX_ROWS == 0 and seq % INPROJ_ROWS == 0 and MIX_ROWS % ATTN_BLOCK == 0
    alpha = (2 * depth) ** 0.25

    mod = _ada_call(c, w_ada, b_ada)
    shift, scale, gate = [mod[:, :, i * d:(i + 1) * d].reshape(depth, batch, 1, d)
                          for i in range(3)]

    stacks = {"in": w_in, "grp": w_pool_grp.reshape(depth, pool_width, -1),
              "pool_up": w_pool_up, "attn_up": w_attn_up, "out": w_out}
    bf16_w = {("in", 0): w_in[0].astype(BF16), ("grp", 0): stacks["grp"][0].astype(BF16)}
    pending = [(name, l) for l in range(depth) for name in stacks if (name, l) not in bf16_w]

    ln = [(ln_g[l].reshape(1, d), ln_b[l].reshape(1, d)) for l in range(depth)]
    xf = x.reshape(batch * seq, d)
    pending_norm = None
    for l in range(depth):
        if pending_norm is None:
            outs = _inproj_call(xf, scale[l], shift[l], bf16_w["in", l], bf16_w["grp", l],
                                pool_scale[l].reshape(1, -1), widths, seq,
                                convert=[(stacks[name], j) for name, j in pending])
            h_parts = outs[:6]
            bf16_w.update(zip(pending, outs[6:]))
        else:
            cen, mean, rstd = pending_norm
            xf, *h_parts = _inproj_call(cen, scale[l], shift[l], bf16_w["in", l],
                                        bf16_w["grp", l], pool_scale[l].reshape(1, -1), widths,
                                        seq, norm=(mean, rstd) + ln[l - 1])
        last = l == depth - 1
        res = _mix_call(h_parts, xf, gate[l], sinks[l], bf16_w["pool_up", l],
                        bf16_w["attn_up", l], bf16_w["out", l], ln[l] if last else None,
                        batch, seq, alpha)
        if last:
            xf = res
        else:
            pending_norm = res
    return xf.reshape(batch, seq, d)
```

```python
import functools

import jax
import jax.numpy as jnp
from jax import lax
from jax.experimental import pallas as pl
from jax.experimental.pallas import tpu as pltpu

F32 = jnp.float32
BF16 = jnp.bfloat16

POOL_WINDOWS = (2, 4, 8, 16)
POOL_GROUP_DIM = 256
MAX_POOL_WINDOW = 16
HEAD_DIM = 64
N_KV_HEADS = 4
Q_PER_KV = 4
ATTN_BLOCK = 128
LN_EPS = 1e-5
LOG2_E = 1.4426950408889634

V7X_VMEM_LIMIT_CAP = 60000 * 1024
BF16_SUBLANES = 16
LANES = 128

ADA_COLS = 1536
INPROJ_ROWS = 256
INPROJ_COLS = 1024
W_LOAD_ROWS = 32
MIX_ROWS = 512
MIX_OUT_COLS = 512


def _vmem_limit(estimate_bytes):
    return int(min(V7X_VMEM_LIMIT_CAP, max(32 * 1024 * 1024, estimate_bytes)))


def _ada_kernel(c_ref, w_ref, b_ref, o_ref):
    sc = jax.nn.silu(c_ref[...]).astype(BF16)
    acc = jnp.dot(sc, w_ref[...].astype(BF16), preferred_element_type=F32)
    o_ref[...] = acc + b_ref[...]


def _ada_call(c, w_ada, b_ada):
    depth, d, n = w_ada.shape
    b = c.shape[0]
    return pl.pallas_call(
        _ada_kernel,
        grid=(depth, n // ADA_COLS),
        in_specs=[
            pl.BlockSpec((b, d), lambda l, j: (0, 0)),
            pl.BlockSpec((None, d, ADA_COLS), lambda l, j: (l, 0, j)),
            pl.BlockSpec((None, 1, ADA_COLS), lambda l, j: (l, 0, j)),
        ],
        out_specs=pl.BlockSpec((None, b, ADA_COLS), lambda l, j: (l, 0, j)),
        out_shape=jax.ShapeDtypeStruct((depth, b, n), F32),
        compiler_params=pltpu.CompilerParams(
            dimension_semantics=("arbitrary", "arbitrary"),
            vmem_limit_bytes=_vmem_limit(2 * d * ADA_COLS * 4 + 8 * 1024 * 1024),
        ),
        name="ada_mod",
    )(c, w_ada, b_ada.reshape(depth, 1, n))


def _pool_window(g, p, prev, pos):
    w = POOL_WINDOWS[g]
    s = jnp.concatenate([prev, p], axis=0)
    span = 1
    while span < w:
        s = s + pltpu.roll(s, span, 0)
        span *= 2
    win_sum = s[MAX_POOL_WINDOW:, :]
    count = jnp.minimum(pos + 1, w).astype(F32)
    return win_sum / count - p


def _inproj_kernel(x_ref, scale_ref, shift_ref, w_ref, wgrp_ref, pscale_ref, *rest,
                   widths, n_convert, tiles_per_seq, norm_input, load_steps):
    if norm_input:
        mu_ref, rstd_ref, lng_ref, lnb_ref = rest[:4]
        rest = rest[4:]
    slabs_in, rest = rest[:n_convert], rest[n_convert:]
    if norm_input:
        xo_ref, rest = rest[0], rest[1:]
    ypool_ref, q_ref, k_ref, v_ref, agate_ref, g_ref = rest[:6]
    slabs_out = rest[6:6 + n_convert]
    u_scr, mixed_scr, tail_scr = rest[6 + n_convert:9 + n_convert]
    rows = x_ref.shape[0]
    step = pl.program_id(0)

    if load_steps:
        (w_bf16,) = rest[9 + n_convert:]
        slab_rows = w_ref.shape[0]

        @pl.when(step < load_steps)
        def _():
            r0 = pl.multiple_of(step * slab_rows, slab_rows)
            w_bf16[pl.ds(r0, slab_rows), :] = w_ref[...].astype(BF16)
    else:
        w_bf16 = w_ref

    def project_tile():
        i = step - load_steps
        t = i % tiles_per_seq
        first_tile = t == 0

        @pl.when(i == 0)
        def _():
            tail_scr[...] = jnp.zeros(tail_scr.shape, tail_scr.dtype)

        if norm_input:
            reps = x_ref.shape[1] // LANES
            mu = jnp.concatenate([mu_ref[...]] * reps, axis=1)
            rstd = jnp.concatenate([rstd_ref[...]] * reps, axis=1)
            x = (x_ref[...] - mu) * rstd * lng_ref[...] + lnb_ref[...]
            xo_ref[...] = x
        else:
            x = x_ref[...]
        u_scr[...] = (x * (1.0 + scale_ref[...]) + shift_ref[...]).astype(BF16)

        def project(c0, c1):
            return jnp.dot(u_scr[...], w_bf16[:, c0:c1], preferred_element_type=F32)

        pool_width = widths[0]
        pos = t * rows + lax.broadcasted_iota(jnp.int32, (rows, 1), 0)
        pool_step = min(INPROJ_COLS, pool_width)
        for c in range(0, pool_width, pool_step):
            acc = project(c, c + pool_step)
            for j in range(pool_step // POOL_GROUP_DIM):
                lo = c + j * POOL_GROUP_DIM
                pm = jnp.dot(acc[:, j * POOL_GROUP_DIM:(j + 1) * POOL_GROUP_DIM].astype(BF16),
                             wgrp_ref[lo:lo + POOL_GROUP_DIM, :], preferred_element_type=F32)
                prev = jnp.where(first_tile, 0.0, tail_scr[:, lo:lo + POOL_GROUP_DIM])
                tail_scr[:, lo:lo + POOL_GROUP_DIM] = pm[rows - MAX_POOL_WINDOW:, :]
                mixed_scr[:, lo:lo + POOL_GROUP_DIM] = _pool_window(
                    lo // POOL_GROUP_DIM, pm, prev, pos)
        col = pool_width
        for c in range(0, pool_width, pool_step):
            acc = project(col + c, col + c + pool_step)
            ypool_ref[:, c:c + pool_step] = (
                mixed_scr[:, c:c + pool_step] * pscale_ref[:, c:c + pool_step]
                * jax.nn.silu(acc)).astype(BF16)
        col += pool_width

        epilogues = (
            (q_ref, lambda a: a * (HEAD_DIM ** -0.5 * LOG2_E)),
            (k_ref, None),
            (v_ref, None),
            (agate_ref, jax.nn.silu),
            (g_ref, jax.nn.sigmoid),
        )
        for (o_ref, fn), width in zip(epilogues, widths[2:]):
            chunk = min(INPROJ_COLS, width)
            for c in range(0, width, chunk):
                acc = project(col + c, col + c + chunk)
                if fn is not None:
                    acc = fn(acc)
                o_ref[:, c:c + chunk] = acc.astype(o_ref.dtype)
            col += width

        for src, dst in zip(slabs_in, slabs_out):
            dst[...] = src[...].astype(BF16)

    if load_steps:
        pl.when(step >= load_steps)(project_tile)
    else:
        project_tile()


def _inproj_call(x, scale, shift, w_in, wgrp, pscale, widths, seq, convert=(), norm=None):
    m, d = x.shape
    steps = m // INPROJ_ROWS
    tiles_per_seq = seq // INPROJ_ROWS
    pool_width = widths[0]
    assert widths[1] == pool_width and pool_width % min(INPROJ_COLS, pool_width) == 0
    full = lambda a: pl.BlockSpec(a.shape, lambda i: (0,) * a.ndim,
                                  pipeline_mode=pl.Buffered(1))
    scratch_shapes = [pltpu.VMEM((INPROJ_ROWS, d), BF16),
                      pltpu.VMEM((INPROJ_ROWS, pool_width), F32),
                      pltpu.VMEM((MAX_POOL_WINDOW, pool_width), F32)]
    if isinstance(w_in, tuple):
        w_stack, w_layer = w_in
        n_in = w_stack.shape[2]
        load_steps = d // W_LOAD_ROWS
        assert load_steps * W_LOAD_ROWS == d and W_LOAD_ROWS % BF16_SUBLANES == 0
        w_arg = w_stack
        w_spec = pl.BlockSpec((None, W_LOAD_ROWS, n_in),
                              lambda i: (w_layer, jnp.minimum(i, load_steps - 1), 0))
        scratch_shapes.append(pltpu.VMEM((d, n_in), BF16))
    else:
        n_in = w_in.shape[1]
        load_steps = 0
        w_arg, w_spec = w_in, full(w_in)
    tile = lambda i: jnp.maximum(i - load_steps, 0)
    row_spec = lambda width: pl.BlockSpec((INPROJ_ROWS, width), lambda i: (tile(i), 0))
    mod_spec = pl.BlockSpec((None, 1, d), lambda i: (tile(i) // tiles_per_seq, 0, 0))
    out_widths = (pool_width,) + tuple(widths[2:])
    out_shape = [jax.ShapeDtypeStruct((m, width), BF16) for width in out_widths]
    out_specs = [row_spec(width) for width in out_widths]
    vmem = (d * n_in * 2 + 2 * INPROJ_ROWS * d * 4 + 2 * INPROJ_ROWS * sum(out_widths) * 2
            + INPROJ_ROWS * d * 2 + INPROJ_ROWS * pool_width * 4 + 8 * 1024 * 1024)
    if load_steps:
        vmem += 2 * W_LOAD_ROWS * n_in * 4
    slab_in_specs, slab_out_specs = [], []
    for a, layer in convert:
        _, r, c = a.shape
        slab = r // steps
        assert slab * steps == r and slab % BF16_SUBLANES == 0
        slab_in_specs.append(
            pl.BlockSpec((None, slab, c), lambda i, layer=layer: (layer, tile(i), 0)))
        slab_out_specs.append(pl.BlockSpec((slab, c), lambda i: (tile(i), 0)))
        out_shape.append(jax.ShapeDtypeStruct((r, c), BF16))
        vmem += 2 * slab * c * (4 + 2)
    norm_specs, norm_args = [], []
    if norm is not None:
        mean, rstd, gain, bias = norm
        norm_specs = [row_spec(LANES), row_spec(LANES), full(gain), full(bias)]
        norm_args = [mean, rstd, gain, bias]
        out_shape.insert(0, jax.ShapeDtypeStruct((m, d), F32))
        out_specs.insert(0, row_spec(d))
        vmem += 2 * INPROJ_ROWS * d * 4
    return pl.pallas_call(
        functools.partial(_inproj_kernel, widths=widths, n_convert=len(convert),
                          tiles_per_seq=tiles_per_seq, norm_input=norm is not None,
                          load_steps=load_steps),
        grid=(load_steps + steps,),
        in_specs=[row_spec(d), mod_spec, mod_spec, w_spec, full(wgrp), full(pscale)]
        + norm_specs + slab_in_specs,
        out_specs=out_specs + slab_out_specs,
        out_shape=out_shape,
        scratch_shapes=scratch_shapes,
        compiler_params=pltpu.CompilerParams(dimension_semantics=("arbitrary",),
                                             vmem_limit_bytes=_vmem_limit(vmem)),
        name="inproj",
    )(x, scale, shift, w_arg, wgrp, pscale, *norm_args, *[a for a, _ in convert])


def _attention(q_ref, k_ref, kprev_ref, v_ref, vprev_ref, sinks_ref, first_tile):
    rows = q_ref.shape[0]
    stack = Q_PER_KV * ATTN_BLOCK
    kext = jnp.concatenate([kprev_ref[...], k_ref[...]], axis=0)
    vext = jnp.concatenate([vprev_ref[...], v_ref[...]], axis=0)
    kj = lax.broadcasted_iota(jnp.int32, (2 * ATTN_BLOCK, stack), 0)
    qi = lax.broadcasted_iota(jnp.int32, (2 * ATTN_BLOCK, stack), 1) & (ATTN_BLOCK - 1)
    band = (kj > qi) & (kj <= qi + ATTN_BLOCK)
    bias_inner = jnp.where(band, 0.0, -jnp.inf).astype(F32)
    bias_first = jnp.where(band & ((kj >= ATTN_BLOCK) | jnp.logical_not(first_tile)),
                           0.0, -jnp.inf).astype(F32)
    blocks = []
    for qb in range(rows // ATTN_BLOCK):
        r0 = qb * ATTN_BLOCK
        bias = bias_first if qb == 0 else bias_inner
        pieces = []
        for h in range(N_KV_HEADS):
            qh = q_ref[r0:r0 + ATTN_BLOCK, h * Q_PER_KV * HEAD_DIM:(h + 1) * Q_PER_KV * HEAD_DIM]
            qs = jnp.concatenate(
                [qh[:, g * HEAD_DIM:(g + 1) * HEAD_DIM] for g in range(Q_PER_KV)], axis=0)
            kh = kext[r0:r0 + 2 * ATTN_BLOCK, h * HEAD_DIM:(h + 1) * HEAD_DIM]
            vh = vext[r0:r0 + 2 * ATTN_BLOCK, h * HEAD_DIM:(h + 1) * HEAD_DIM]
            st = lax.dot_general(kh, qs, (((1,), (1,)), ((), ())),
                                 preferred_element_type=F32) + bias
            sink = LOG2_E * jnp.concatenate(
                [jnp.full((1, ATTN_BLOCK), sinks_ref[h * Q_PER_KV + g], F32)
                 for g in range(Q_PER_KV)], axis=1)
            mx = jnp.maximum(jnp.max(st, axis=0, keepdims=True), sink)
            e = jnp.exp2(st - mx)
            denom = jnp.sum(e, axis=0, keepdims=True) + jnp.exp2(sink - mx)
            probs = (e * (1.0 / denom)).astype(BF16)
            o = lax.dot_general(probs, vh, (((0,), (0,)), ((), ())),
                                preferred_element_type=F32)
            pieces.extend(o[g * ATTN_BLOCK:(g + 1) * ATTN_BLOCK] for g in range(Q_PER_KV))
        blocks.append(jnp.concatenate(pieces, axis=1))
    return jnp.concatenate(blocks, axis=0)


def _mix_kernel(sinks_ref, ypool_ref, q_ref, k_ref, kprev_ref, v_ref, vprev_ref, agate_ref,
                g_ref, x_ref, gate_ref, wpu_ref, wau_ref, wout_ref, *rest, alpha, apply_norm):
    if apply_norm:
        lng_ref, lnb_ref, xo_ref, yattn_scr, merged_scr, cen_ref = rest
    else:
        cen_ref, mu_ref, rstd_ref, yattn_scr, merged_scr = rest
    rows, d = x_ref.shape
    first_tile = pl.program_id(1) == 0

    attn = _attention(q_ref, k_ref, kprev_ref, v_ref, vprev_ref, sinks_ref, first_tile)
    y_attn = (attn * agate_ref[...].astype(F32)).astype(BF16)

    yattn_scr[...] = y_attn
    for lo in range(0, d, MIX_OUT_COLS):
        hi = lo + MIX_OUT_COLS
        up_pool = jnp.dot(ypool_ref[...], wpu_ref[:, lo:hi], preferred_element_type=F32)
        up_attn = jnp.dot(yattn_scr[...], wau_ref[:, lo:hi], preferred_element_type=F32)
        merged_scr[:, lo:hi] = (g_ref[:, lo:hi].astype(F32) * up_pool
                                + g_ref[:, d + lo:d + hi].astype(F32) * up_attn).astype(BF16)

    lanes = LANES
    s1 = jnp.zeros((rows, lanes), F32)
    s2 = jnp.zeros((rows, lanes), F32)
    shift = None
    for lo in range(0, d, MIX_OUT_COLS):
        hi = lo + MIX_OUT_COLS
        out = jnp.dot(merged_scr[...], wout_ref[:, lo:hi], preferred_element_type=F32)
        y = alpha * x_ref[:, lo:hi] + gate_ref[:, lo:hi] * out
        if shift is None:
            shift = jnp.mean(y, axis=-1, keepdims=True)
        cen = y - shift
        cen_ref[:, lo:hi] = cen
        for j in range(0, MIX_OUT_COLS, lanes):
            piece = cen[:, j:j + lanes]
            s1 = s1 + piece
            s2 = s2 + piece * piece
    mu = jnp.sum(s1, axis=-1, keepdims=True) * (1.0 / d)
    var = jnp.sum(s2, axis=-1, keepdims=True) * (1.0 / d) - mu * mu
    rstd = lax.rsqrt(var + LN_EPS)
    if apply_norm:
        xo_ref[...] = (cen_ref[...] - mu) * rstd * lng_ref[...] + lnb_ref[...]
    else:
        mu_ref[...] = jnp.broadcast_to(mu, mu_ref.shape)
        rstd_ref[...] = jnp.broadcast_to(rstd, rstd_ref.shape)


def _mix_call(h_parts, x, gate, sinks, wpu, wau, wout, ln, batch, seq, alpha):
    ypool, q, k, v, agate, g = h_parts
    m, d = x.shape
    nt = seq // MIX_ROWS
    apply_norm = ln is not None

    row = lambda width: pl.BlockSpec((MIX_ROWS, width), lambda b, t: (b * nt + t, 0))

    def prev_spec(width):
        per_tile, per_seq = MIX_ROWS // ATTN_BLOCK, seq // ATTN_BLOCK
        return pl.BlockSpec(
            (ATTN_BLOCK, width),
            lambda b, t: (b * per_seq + jnp.maximum(t * per_tile - 1, 0), 0))

    full = lambda a: pl.BlockSpec(a.shape, lambda b, t: (0,) * a.ndim,
                                  pipeline_mode=pl.Buffered(1))
    mod_spec = pl.BlockSpec((None, 1, d), lambda b, t: (b, 0, 0))

    in_specs = [
        pl.BlockSpec(memory_space=pltpu.SMEM),
        row(ypool.shape[1]), row(q.shape[1]),
        row(k.shape[1]), prev_spec(k.shape[1]),
        row(v.shape[1]), prev_spec(v.shape[1]),
        row(agate.shape[1]), row(g.shape[1]), row(d), mod_spec,
        full(wpu), full(wau), full(wout),
    ]
    args = [sinks, ypool, q, k, k, v, v, agate, g, x, gate, wpu, wau, wout]
    scratch_shapes = [pltpu.VMEM((MIX_ROWS, q.shape[1]), BF16),
                      pltpu.VMEM((MIX_ROWS, d), BF16)]
    if apply_norm:
        in_specs += [full(ln[0]), full(ln[1])]
        args += list(ln)
        out_specs = row(d)
        out_shape = jax.ShapeDtypeStruct((m, d), F32)
        scratch_shapes.append(pltpu.VMEM((MIX_ROWS, d), F32))
    else:
        out_specs = [row(d), row(LANES), row(LANES)]
        out_shape = [jax.ShapeDtypeStruct((m, d), F32), jax.ShapeDtypeStruct((m, LANES), F32),
                     jax.ShapeDtypeStruct((m, LANES), F32)]

    h_cols = sum(a.shape[1] for a in h_parts)
    weights = 2 * (wpu.size + wau.size + wout.size)
    tiles = 2 * MIX_ROWS * (h_cols * 2 + d * 4 + d * 4)
    temps = 10 * MIX_ROWS * d * 4
    return pl.pallas_call(
        functools.partial(_mix_kernel, alpha=alpha, apply_norm=apply_norm),
        grid=(batch, nt),
        in_specs=in_specs,
        out_specs=out_specs,
        out_shape=out_shape,
        scratch_shapes=scratch_shapes,
        compiler_params=pltpu.CompilerParams(
            dimension_semantics=("arbitrary", "arbitrary"),
            vmem_limit_bytes=_vmem_limit(weights + tiles + temps),
        ),
        name="mixer",
    )(*args)


def kernel(x, c, w_ada, b_ada, w_in, w_pool_grp, pool_scale, sinks, w_pool_up, w_attn_up,
           w_out, ln_g, ln_b):
    batch, seq, d = x.shape
    depth = w_ada.shape[0]
    pool_width = w_pool_grp.shape[1] * w_pool_grp.shape[2]
    attn_width = w_attn_up.shape[1]
    kv_width = N_KV_HEADS * HEAD_DIM
    widths = (pool_width, pool_width, attn_width, kv_width, kv_width, attn_width, 2 * d)
    assert sum(widths) == w_in.shape[2]
    assert seq % MIX_ROWS == 0 and seq % INPROJ_ROWS == 0 and MIX_ROWS % ATTN_BLOCK == 0
    alpha = (2 * depth) ** 0.25

    mod = _ada_call(c, w_ada, b_ada)
    shift, scale, gate = [mod[:, :, i * d:(i + 1) * d].reshape(depth, batch, 1, d)
                          for i in range(3)]

    stacks = {"in": w_in, "grp": w_pool_grp.reshape(depth, pool_width, -1),
              "pool_up": w_pool_up, "attn_up": w_attn_up, "out": w_out}
    bf16_w = {("grp", 0): stacks["grp"][0].astype(BF16)}
    first_w_in = ("in", 0)
    pending = [(name, l) for l in range(depth) for name in stacks
               if (name, l) not in bf16_w and (name, l) != first_w_in]

    ln = [(ln_g[l].reshape(1, d), ln_b[l].reshape(1, d)) for l in range(depth)]
    xf = x.reshape(batch * seq, d)
    pending_norm = None
    for l in range(depth):
        if pending_norm is None:
            outs = _inproj_call(xf, scale[l], shift[l], (w_in, l), bf16_w["grp", l],
                                pool_scale[l].reshape(1, -1), widths, seq,
                                convert=[(stacks[name], j) for name, j in pending])
            h_parts = outs[:6]
            bf16_w.update(zip(pending, outs[6:]))
        else:
            cen, mean, rstd = pending_norm
            xf, *h_parts = _inproj_call(cen, scale[l], shift[l], bf16_w["in", l],
                                        bf16_w["grp", l], pool_scale[l].reshape(1, -1), widths,
                                        seq, norm=(mean, rstd) + ln[l - 1])
        last = l == depth - 1
        res = _mix_call(h_parts, xf, gate[l], sinks[l], bf16_w["pool_up", l],
                        bf16_w["attn_up", l], bf16_w["out", l], ln[l] if last else None,
                        batch, seq, alpha)
        if last:
            xf = res
        else:
            pending_norm = res
    return xf.reshape(batch, seq, d)
```

```python
import functools

import jax
import jax.numpy as jnp
from jax import lax
from jax.experimental import pallas as pl
from jax.experimental.pallas import tpu as pltpu

F32 = jnp.float32
BF16 = jnp.bfloat16

POOL_WINDOWS = (2, 4, 8, 16)
POOL_GROUP_DIM = 256
MAX_POOL_WINDOW = 16
HEAD_DIM = 64
N_KV_HEADS = 4
Q_PER_KV = 4
ATTN_BLOCK = 128
LN_EPS = 1e-5
LOG2_E = 1.4426950408889634

MIB = 1024 * 1024
V7X_VMEM_LIMIT_CAP = 60000 * 1024
V7X_VMEM_LIMIT_DEFAULT = 32 * MIB
BF16_SUBLANES = 16
LANES = 128
TEMPORARIES_BYTES = 8 * MIB

ADA_COLS = 1536
INPROJ_ROWS = 256
INPROJ_COLS = 1024
W_LOAD_ROWS = BF16_SUBLANES
W_LOAD_BUFFERS = 4
MIX_ROWS = 512
MIX_OUT_COLS = 512


def _vmem_limit(estimate_bytes):
    return int(min(V7X_VMEM_LIMIT_CAP, max(V7X_VMEM_LIMIT_DEFAULT, estimate_bytes)))


def _ada_kernel(c_ref, w_ref, b_ref, o_ref):
    sc = jax.nn.silu(c_ref[...]).astype(BF16)
    acc = jnp.dot(sc, w_ref[...].astype(BF16), preferred_element_type=F32)
    o_ref[...] = acc + b_ref[...]


def _ada_call(c, w_ada, b_ada):
    depth, d, n = w_ada.shape
    b = c.shape[0]
    return pl.pallas_call(
        _ada_kernel,
        grid=(depth, n // ADA_COLS),
        in_specs=[
            pl.BlockSpec((b, d), lambda l, j: (0, 0)),
            pl.BlockSpec((None, d, ADA_COLS), lambda l, j: (l, 0, j)),
            pl.BlockSpec((None, 1, ADA_COLS), lambda l, j: (l, 0, j)),
        ],
        out_specs=pl.BlockSpec((None, b, ADA_COLS), lambda l, j: (l, 0, j)),
        out_shape=jax.ShapeDtypeStruct((depth, b, n), F32),
        compiler_params=pltpu.CompilerParams(
            dimension_semantics=("arbitrary", "arbitrary"),
            vmem_limit_bytes=_vmem_limit(2 * d * ADA_COLS * 4 + TEMPORARIES_BYTES),
        ),
        name="ada_mod",
    )(c, w_ada, b_ada.reshape(depth, 1, n))


def _pool_window(g, p, prev, pos):
    w = POOL_WINDOWS[g]
    s = jnp.concatenate([prev, p], axis=0)
    span = 1
    while span < w:
        s = s + pltpu.roll(s, span, 0)
        span *= 2
    win_sum = s[MAX_POOL_WINDOW:, :]
    count = jnp.minimum(pos + 1, w).astype(F32)
    return win_sum / count - p


def _load_weight(w_hbm, layer, w_scr, stage_scr, sems):
    n_buf, chunk, _ = stage_scr.shape
    n_chunks = w_scr.shape[0] // chunk

    def copy(c):
        slot = c % n_buf
        return pltpu.make_async_copy(w_hbm.at[layer, pl.ds(c * chunk, chunk), :],
                                     stage_scr.at[slot], sems.at[slot])

    for c in range(n_buf - 1):
        copy(c).start()

    def body(c, carry):
        @pl.when(c + n_buf - 1 < n_chunks)
        def _():
            copy(c + n_buf - 1).start()
        copy(c).wait()
        w_scr[pl.ds(pl.multiple_of(c * chunk, chunk), chunk), :] = (
            stage_scr[c % n_buf].astype(BF16))
        return carry

    lax.fori_loop(0, n_chunks, body, 0)


def _inproj_kernel(x_ref, scale_ref, shift_ref, w_ref, wgrp_ref, pscale_ref, *rest,
                   widths, n_convert, tiles_per_seq, norm_input, hbm_weight_layer):
    if norm_input:
        mu_ref, rstd_ref, lng_ref, lnb_ref = rest[:4]
        rest = rest[4:]
    slabs_in, rest = rest[:n_convert], rest[n_convert:]
    if norm_input:
        xo_ref, rest = rest[0], rest[1:]
    ypool_ref, q_ref, k_ref, v_ref, agate_ref, g_ref = rest[:6]
    slabs_out = rest[6:6 + n_convert]
    u_scr, mixed_scr, tail_scr, *load_scr = rest[6 + n_convert:]
    rows = x_ref.shape[0]
    i = pl.program_id(0)
    t = i % tiles_per_seq
    first_tile = t == 0

    @pl.when(i == 0)
    def _():
        tail_scr[...] = jnp.zeros(tail_scr.shape, tail_scr.dtype)
        if hbm_weight_layer is not None:
            _load_weight(w_ref, hbm_weight_layer, *load_scr)

    if hbm_weight_layer is not None:
        w_ref = load_scr[0]

    if norm_input:
        reps = x_ref.shape[1] // LANES
        mu = jnp.concatenate([mu_ref[...]] * reps, axis=1)
        rstd = jnp.concatenate([rstd_ref[...]] * reps, axis=1)
        x = (x_ref[...] - mu) * rstd * lng_ref[...] + lnb_ref[...]
        xo_ref[...] = x
    else:
        x = x_ref[...]
    u_scr[...] = (x * (1.0 + scale_ref[...]) + shift_ref[...]).astype(BF16)

    def project(c0, c1):
        return jnp.dot(u_scr[...], w_ref[:, c0:c1], preferred_element_type=F32)

    pool_width = widths[0]
    pos = t * rows + lax.broadcasted_iota(jnp.int32, (rows, 1), 0)
    pool_step = min(INPROJ_COLS, pool_width)
    for c in range(0, pool_width, pool_step):
        acc = project(c, c + pool_step)
        for j in range(pool_step // POOL_GROUP_DIM):
            lo = c + j * POOL_GROUP_DIM
            pm = jnp.dot(acc[:, j * POOL_GROUP_DIM:(j + 1) * POOL_GROUP_DIM].astype(BF16),
                         wgrp_ref[lo:lo + POOL_GROUP_DIM, :], preferred_element_type=F32)
            prev = jnp.where(first_tile, 0.0, tail_scr[:, lo:lo + POOL_GROUP_DIM])
            tail_scr[:, lo:lo + POOL_GROUP_DIM] = pm[rows - MAX_POOL_WINDOW:, :]
            mixed_scr[:, lo:lo + POOL_GROUP_DIM] = _pool_window(
                lo // POOL_GROUP_DIM, pm, prev, pos)
    col = pool_width
    for c in range(0, pool_width, pool_step):
        acc = project(col + c, col + c + pool_step)
        ypool_ref[:, c:c + pool_step] = (
            mixed_scr[:, c:c + pool_step] * pscale_ref[:, c:c + pool_step]
            * jax.nn.silu(acc)).astype(BF16)
    col += pool_width

    epilogues = (
        (q_ref, lambda a: a * (HEAD_DIM ** -0.5 * LOG2_E)),
        (k_ref, None),
        (v_ref, None),
        (agate_ref, jax.nn.silu),
        (g_ref, jax.nn.sigmoid),
    )
    for (o_ref, fn), width in zip(epilogues, widths[2:]):
        step = min(INPROJ_COLS, width)
        for c in range(0, width, step):
            acc = project(col + c, col + c + step)
            if fn is not None:
                acc = fn(acc)
            o_ref[:, c:c + step] = acc.astype(o_ref.dtype)
        col += width

    for src, dst in zip(slabs_in, slabs_out):
        dst[...] = src[...].astype(BF16)


def _inproj_call(x, scale, shift, w_in, wgrp, pscale, widths, seq, convert=(), norm=None,
                 hbm_weight_layer=None):
    m, d = x.shape
    n_in = w_in.shape[-1]
    steps = m // INPROJ_ROWS
    tiles_per_seq = seq // INPROJ_ROWS
    pool_width = widths[0]
    assert widths[1] == pool_width and pool_width % min(INPROJ_COLS, pool_width) == 0
    row_spec = lambda width: pl.BlockSpec((INPROJ_ROWS, width), lambda i: (i, 0))
    full = lambda a: pl.BlockSpec(a.shape, lambda i: (0,) * a.ndim,
                                  pipeline_mode=pl.Buffered(1))
    mod_spec = pl.BlockSpec((None, 1, d), lambda i: (i // tiles_per_seq, 0, 0))
    out_widths = (pool_width,) + tuple(widths[2:])
    out_shape = [jax.ShapeDtypeStruct((m, width), BF16) for width in out_widths]
    out_specs = [row_spec(width) for width in out_widths]
    vmem = (d * n_in * 2 + 2 * INPROJ_ROWS * d * 4 + 2 * INPROJ_ROWS * sum(out_widths) * 2
            + INPROJ_ROWS * d * 2 + INPROJ_ROWS * pool_width * 4 + TEMPORARIES_BYTES)
    slab_in_specs, slab_out_specs = [], []
    for a, layer in convert:
        _, r, c = a.shape
        slab = r // steps
        assert slab * steps == r and slab % BF16_SUBLANES == 0
        slab_in_specs.append(pl.BlockSpec((None, slab, c), lambda i, layer=layer: (layer, i, 0)))
        slab_out_specs.append(pl.BlockSpec((slab, c), lambda i: (i, 0)))
        out_shape.append(jax.ShapeDtypeStruct((r, c), BF16))
        vmem += 2 * slab * c * (4 + 2)
    norm_specs, norm_args = [], []
    if norm is not None:
        mean, rstd, gain, bias = norm
        norm_specs = [row_spec(LANES), row_spec(LANES), full(gain), full(bias)]
        norm_args = [mean, rstd, gain, bias]
        out_shape.insert(0, jax.ShapeDtypeStruct((m, d), F32))
        out_specs.insert(0, row_spec(d))
        vmem += 2 * INPROJ_ROWS * d * 4
    scratch = [pltpu.VMEM((INPROJ_ROWS, d), BF16),
               pltpu.VMEM((INPROJ_ROWS, pool_width), F32),
               pltpu.VMEM((MAX_POOL_WINDOW, pool_width), F32)]
    if hbm_weight_layer is None:
        w_spec = full(w_in)
    else:
        assert d % W_LOAD_ROWS == 0 and d // W_LOAD_ROWS >= W_LOAD_BUFFERS
        w_spec = pl.BlockSpec(memory_space=pl.ANY)
        scratch += [pltpu.VMEM((d, n_in), BF16),
                    pltpu.VMEM((W_LOAD_BUFFERS, W_LOAD_ROWS, n_in), F32),
                    pltpu.SemaphoreType.DMA((W_LOAD_BUFFERS,))]
        vmem += W_LOAD_BUFFERS * W_LOAD_ROWS * n_in * 4
    return pl.pallas_call(
        functools.partial(_inproj_kernel, widths=widths, n_convert=len(convert),
                          tiles_per_seq=tiles_per_seq, norm_input=norm is not None,
                          hbm_weight_layer=hbm_weight_layer),
        grid=(steps,),
        in_specs=[row_spec(d), mod_spec, mod_spec, w_spec, full(wgrp), full(pscale)]
        + norm_specs + slab_in_specs,
        out_specs=out_specs + slab_out_specs,
        out_shape=out_shape,
        scratch_shapes=scratch,
        compiler_params=pltpu.CompilerParams(dimension_semantics=("arbitrary",),
                                             vmem_limit_bytes=_vmem_limit(vmem)),
        name="inproj",
    )(x, scale, shift, w_in, wgrp, pscale, *norm_args, *[a for a, _ in convert])


def _attention(q_ref, k_ref, kprev_ref, v_ref, vprev_ref, sinks_ref, first_tile):
    rows = q_ref.shape[0]
    stack = Q_PER_KV * ATTN_BLOCK
    kext = jnp.concatenate([kprev_ref[...], k_ref[...]], axis=0)
    vext = jnp.concatenate([vprev_ref[...], v_ref[...]], axis=0)
    kj = lax.broadcasted_iota(jnp.int32, (2 * ATTN_BLOCK, stack), 0)
    qi = lax.broadcasted_iota(jnp.int32, (2 * ATTN_BLOCK, stack), 1) & (ATTN_BLOCK - 1)
    band = (kj > qi) & (kj <= qi + ATTN_BLOCK)
    bias_inner = jnp.where(band, 0.0, -jnp.inf).astype(F32)
    bias_first = jnp.where(band & ((kj >= ATTN_BLOCK) | jnp.logical_not(first_tile)),
                           0.0, -jnp.inf).astype(F32)
    blocks = []
    for qb in range(rows // ATTN_BLOCK):
        r0 = qb * ATTN_BLOCK
        bias = bias_first if qb == 0 else bias_inner
        pieces = []
        for h in range(N_KV_HEADS):
            qh = q_ref[r0:r0 + ATTN_BLOCK, h * Q_PER_KV * HEAD_DIM:(h + 1) * Q_PER_KV * HEAD_DIM]
            qs = jnp.concatenate(
                [qh[:, g * HEAD_DIM:(g + 1) * HEAD_DIM] for g in range(Q_PER_KV)], axis=0)
            kh = kext[r0:r0 + 2 * ATTN_BLOCK, h * HEAD_DIM:(h + 1) * HEAD_DIM]
            vh = vext[r0:r0 + 2 * ATTN_BLOCK, h * HEAD_DIM:(h + 1) * HEAD_DIM]
            st = lax.dot_general(kh, qs, (((1,), (1,)), ((), ())),
                                 preferred_element_type=F32) + bias
            sink = LOG2_E * jnp.concatenate(
                [jnp.full((1, ATTN_BLOCK), sinks_ref[h * Q_PER_KV + g], F32)
                 for g in range(Q_PER_KV)], axis=1)
            mx = jnp.maximum(jnp.max(st, axis=0, keepdims=True), sink)
            e = jnp.exp2(st - mx)
            denom = jnp.sum(e, axis=0, keepdims=True) + jnp.exp2(sink - mx)
            probs = (e * (1.0 / denom)).astype(BF16)
            o = lax.dot_general(probs, vh, (((0,), (0,)), ((), ())),
                                preferred_element_type=F32)
            pieces.extend(o[g * ATTN_BLOCK:(g + 1) * ATTN_BLOCK] for g in range(Q_PER_KV))
        blocks.append(jnp.concatenate(pieces, axis=1))
    return jnp.concatenate(blocks, axis=0)


def _mix_kernel(sinks_ref, ypool_ref, q_ref, k_ref, kprev_ref, v_ref, vprev_ref, agate_ref,
                g_ref, x_ref, gate_ref, wpu_ref, wau_ref, wout_ref, *rest, alpha, apply_norm):
    if apply_norm:
        lng_ref, lnb_ref, xo_ref, yattn_scr, merged_scr, cen_ref = rest
    else:
        cen_ref, mu_ref, rstd_ref, yattn_scr, merged_scr = rest
    rows, d = x_ref.shape
    first_tile = pl.program_id(1) == 0

    attn = _attention(q_ref, k_ref, kprev_ref, v_ref, vprev_ref, sinks_ref, first_tile)
    y_attn = (attn * agate_ref[...].astype(F32)).astype(BF16)

    yattn_scr[...] = y_attn
    for lo in range(0, d, MIX_OUT_COLS):
        hi = lo + MIX_OUT_COLS
        up_pool = jnp.dot(ypool_ref[...], wpu_ref[:, lo:hi], preferred_element_type=F32)
        up_attn = jnp.dot(yattn_scr[...], wau_ref[:, lo:hi], preferred_element_type=F32)
        merged_scr[:, lo:hi] = (g_ref[:, lo:hi].astype(F32) * up_pool
                                + g_ref[:, d + lo:d + hi].astype(F32) * up_attn).astype(BF16)

    lanes = LANES
    s1 = jnp.zeros((rows, lanes), F32)
    s2 = jnp.zeros((rows, lanes), F32)
    shift = None
    for lo in range(0, d, MIX_OUT_COLS):
        hi = lo + MIX_OUT_COLS
        out = jnp.dot(merged_scr[...], wout_ref[:, lo:hi], preferred_element_type=F32)
        y = alpha * x_ref[:, lo:hi] + gate_ref[:, lo:hi] * out
        if shift is None:
            shift = jnp.mean(y, axis=-1, keepdims=True)
        cen = y - shift
        cen_ref[:, lo:hi] = cen
        for j in range(0, MIX_OUT_COLS, lanes):
            piece = cen[:, j:j + lanes]
            s1 = s1 + piece
            s2 = s2 + piece * piece
    mu = jnp.sum(s1, axis=-1, keepdims=True) * (1.0 / d)
    var = jnp.sum(s2, axis=-1, keepdims=True) * (1.0 / d) - mu * mu
    rstd = lax.rsqrt(var + LN_EPS)
    if apply_norm:
        xo_ref[...] = (cen_ref[...] - mu) * rstd * lng_ref[...] + lnb_ref[...]
    else:
        mu_ref[...] = jnp.broadcast_to(mu, mu_ref.shape)
        rstd_ref[...] = jnp.broadcast_to(rstd, rstd_ref.shape)


def _mix_call(h_parts, x, gate, sinks, wpu, wau, wout, ln, batch, seq, alpha):
    ypool, q, k, v, agate, g = h_parts
    m, d = x.shape
    nt = seq // MIX_ROWS
    apply_norm = ln is not None

    row = lambda width: pl.BlockSpec((MIX_ROWS, width), lambda b, t: (b * nt + t, 0))

    def prev_spec(width):
        per_tile, per_seq = MIX_ROWS // ATTN_BLOCK, seq // ATTN_BLOCK
        return pl.BlockSpec(
            (ATTN_BLOCK, width),
            lambda b, t: (b * per_seq + jnp.maximum(t * per_tile - 1, 0), 0))

    full = lambda a: pl.BlockSpec(a.shape, lambda b, t: (0,) * a.ndim,
                                  pipeline_mode=pl.Buffered(1))
    mod_spec = pl.BlockSpec((None, 1, d), lambda b, t: (b, 0, 0))

    in_specs = [
        pl.BlockSpec(memory_space=pltpu.SMEM),
        row(ypool.shape[1]), row(q.shape[1]),
        row(k.shape[1]), prev_spec(k.shape[1]),
        row(v.shape[1]), prev_spec(v.shape[1]),
        row(agate.shape[1]), row(g.shape[1]), row(d), mod_spec,
        full(wpu), full(wau), full(wout),
    ]
    args = [sinks, ypool, q, k, k, v, v, agate, g, x, gate, wpu, wau, wout]
    scratch_shapes = [pltpu.VMEM((MIX_ROWS, q.shape[1]), BF16),
                      pltpu.VMEM((MIX_ROWS, d), BF16)]
    if apply_norm:
        in_specs += [full(ln[0]), full(ln[1])]
        args += list(ln)
        out_specs = row(d)
        out_shape = jax.ShapeDtypeStruct((m, d), F32)
        scratch_shapes.append(pltpu.VMEM((MIX_ROWS, d), F32))
    else:
        out_specs = [row(d), row(LANES), row(LANES)]
        out_shape = [jax.ShapeDtypeStruct((m, d), F32), jax.ShapeDtypeStruct((m, LANES), F32),
                     jax.ShapeDtypeStruct((m, LANES), F32)]

    h_cols = sum(a.shape[1] for a in h_parts)
    weights = 2 * (wpu.size + wau.size + wout.size)
    tiles = 2 * MIX_ROWS * (h_cols * 2 + d * 4 + d * 4)
    temps = 10 * MIX_ROWS * d * 4
    return pl.pallas_call(
        functools.partial(_mix_kernel, alpha=alpha, apply_norm=apply_norm),
        grid=(batch, nt),
        in_specs=in_specs,
        out_specs=out_specs,
        out_shape=out_shape,
        scratch_shapes=scratch_shapes,
        compiler_params=pltpu.CompilerParams(
            dimension_semantics=("arbitrary", "arbitrary"),
            vmem_limit_bytes=_vmem_limit(weights + tiles + temps),
        ),
        name="mixer",
    )(*args)


def kernel(x, c, w_ada, b_ada, w_in, w_pool_grp, pool_scale, sinks, w_pool_up, w_attn_up,
           w_out, ln_g, ln_b):
    batch, seq, d = x.shape
    depth = w_ada.shape[0]
    pool_width = w_pool_grp.shape[1] * w_pool_grp.shape[2]
    attn_width = w_attn_up.shape[1]
    kv_width = N_KV_HEADS * HEAD_DIM
    widths = (pool_width, pool_width, attn_width, kv_width, kv_width, attn_width, 2 * d)
    assert sum(widths) == w_in.shape[2]
    assert seq % MIX_ROWS == 0 and seq % INPROJ_ROWS == 0 and MIX_ROWS % ATTN_BLOCK == 0
    alpha = (2 * depth) ** 0.25

    mod = _ada_call(c, w_ada, b_ada)
    shift, scale, gate = [mod[:, :, i * d:(i + 1) * d].reshape(depth, batch, 1, d)
                          for i in range(3)]

    stacks = {"in": w_in, "grp": w_pool_grp.reshape(depth, pool_width, -1),
              "pool_up": w_pool_up, "attn_up": w_attn_up, "out": w_out}
    bf16_w = {("grp", 0): stacks["grp"][0].astype(BF16)}
    pending = [(name, l) for l in range(depth) for name in stacks
               if (name, l) not in bf16_w and (name, l) != ("in", 0)]

    ln = [(ln_g[l].reshape(1, d), ln_b[l].reshape(1, d)) for l in range(depth)]
    xf = x.reshape(batch * seq, d)
    pending_norm = None
    for l in range(depth):
        if pending_norm is None:
            outs = _inproj_call(xf, scale[l], shift[l], w_in, bf16_w["grp", l],
                                pool_scale[l].reshape(1, -1), widths, seq,
                                convert=[(stacks[name], j) for name, j in pending],
                                hbm_weight_layer=l)
            h_parts = outs[:6]
            bf16_w.update(zip(pending, outs[6:]))
        else:
            cen, mean, rstd = pending_norm
            xf, *h_parts = _inproj_call(cen, scale[l], shift[l], bf16_w["in", l],
                                        bf16_w["grp", l], pool_scale[l].reshape(1, -1), widths,
                                        seq, norm=(mean, rstd) + ln[l - 1])
        last = l == depth - 1
        res = _mix_call(h_parts, xf, gate[l], sinks[l], bf16_w["pool_up", l],
                        bf16_w["attn_up", l], bf16_w["out", l], ln[l] if last else None,
                        batch, seq, alpha)
        if last:
            xf = res
        else:
            pending_norm = res
    return xf.reshape(batch, seq, d)
```

```python
import functools

import jax
import jax.numpy as jnp
from jax import lax
from jax.experimental import pallas as pl
from jax.experimental.pallas import tpu as pltpu

F32 = jnp.float32
BF16 = jnp.bfloat16

POOL_WINDOWS = (2, 4, 8, 16)
POOL_GROUP_DIM = 256
MAX_POOL_WINDOW = 16
HEAD_DIM = 64
N_KV_HEADS = 4
Q_PER_KV = 4
ATTN_BLOCK = 128
LN_EPS = 1e-5
LOG2_E = 1.4426950408889634

MIB = 1024 * 1024
V7X_VMEM_LIMIT_CAP = 60000 * 1024
V7X_VMEM_LIMIT_DEFAULT = 32 * MIB
BF16_SUBLANES = 16
LANES = 128
TEMPORARIES_BYTES = 8 * MIB

ADA_COLS = 1536
INPROJ_ROWS = 256
INPROJ_COLS = 1024
W_LOAD_ROWS = BF16_SUBLANES
W_LOAD_BUFFERS = 5
MIX_ROWS = 512
MIX_OUT_COLS = 512


def _vmem_limit(estimate_bytes):
    return int(min(V7X_VMEM_LIMIT_CAP, max(V7X_VMEM_LIMIT_DEFAULT, estimate_bytes)))


def _ada_kernel(c_ref, w_ref, b_ref, o_ref):
    sc = jax.nn.silu(c_ref[...]).astype(BF16)
    acc = jnp.dot(sc, w_ref[...].astype(BF16), preferred_element_type=F32)
    o_ref[...] = acc + b_ref[...]


def _ada_call(c, w_ada, b_ada):
    depth, d, n = w_ada.shape
    b = c.shape[0]
    return pl.pallas_call(
        _ada_kernel,
        grid=(depth, n // ADA_COLS),
        in_specs=[
            pl.BlockSpec((b, d), lambda l, j: (0, 0)),
            pl.BlockSpec((None, d, ADA_COLS), lambda l, j: (l, 0, j)),
            pl.BlockSpec((None, 1, ADA_COLS), lambda l, j: (l, 0, j)),
        ],
        out_specs=pl.BlockSpec((None, b, ADA_COLS), lambda l, j: (l, 0, j)),
        out_shape=jax.ShapeDtypeStruct((depth, b, n), F32),
        compiler_params=pltpu.CompilerParams(
            dimension_semantics=("arbitrary", "arbitrary"),
            vmem_limit_bytes=_vmem_limit(2 * d * ADA_COLS * 4 + TEMPORARIES_BYTES),
        ),
        name="ada_mod",
    )(c, w_ada, b_ada.reshape(depth, 1, n))


def _pool_window(g, p, prev, pos):
    w = POOL_WINDOWS[g]
    s = jnp.concatenate([prev, p], axis=0)
    span = 1
    while span < w:
        s = s + pltpu.roll(s, span, 0)
        span *= 2
    win_sum = s[MAX_POOL_WINDOW:, :]
    count = jnp.minimum(pos + 1, w).astype(F32)
    return win_sum / count - p


def _load_weight(w_hbm, layer, w_scr, stage_scr, sems):
    n_buf, chunk, _ = stage_scr.shape
    n_chunks = w_scr.shape[0] // chunk

    def copy(c):
        slot = c % n_buf
        return pltpu.make_async_copy(w_hbm.at[layer, pl.ds(c * chunk, chunk), :],
                                     stage_scr.at[slot], sems.at[slot])

    for c in range(n_buf - 1):
        copy(c).start()

    def body(c, carry):
        @pl.when(c + n_buf - 1 < n_chunks)
        def _():
            copy(c + n_buf - 1).start()
        copy(c).wait()
        w_scr[pl.ds(pl.multiple_of(c * chunk, chunk), chunk), :] = (
            stage_scr[c % n_buf].astype(BF16))
        return carry

    lax.fori_loop(0, n_chunks, body, 0)


def _inproj_kernel(x_ref, scale_ref, shift_ref, w_ref, wgrp_ref, pscale_ref, *rest,
                   widths, n_convert, tiles_per_seq, norm_input, hbm_weight_layer):
    if norm_input:
        mu_ref, rstd_ref, lng_ref, lnb_ref = rest[:4]
        rest = rest[4:]
    slabs_in, rest = rest[:n_convert], rest[n_convert:]
    if norm_input:
        xo_ref, rest = rest[0], rest[1:]
    ypool_ref, q_ref, k_ref, v_ref, agate_ref, g_ref = rest[:6]
    slabs_out = rest[6:6 + n_convert]
    u_scr, mixed_scr, tail_scr, *load_scr = rest[6 + n_convert:]
    rows = x_ref.shape[0]
    i = pl.program_id(0)
    t = i % tiles_per_seq
    first_tile = t == 0

    @pl.when(i == 0)
    def _():
        tail_scr[...] = jnp.zeros(tail_scr.shape, tail_scr.dtype)
        if hbm_weight_layer is not None:
            _load_weight(w_ref, hbm_weight_layer, *load_scr)

    if hbm_weight_layer is not None:
        w_ref = load_scr[0]

    if norm_input:
        reps = x_ref.shape[1] // LANES
        mu = jnp.concatenate([mu_ref[...]] * reps, axis=1)
        rstd = jnp.concatenate([rstd_ref[...]] * reps, axis=1)
        x = (x_ref[...] - mu) * rstd * lng_ref[...] + lnb_ref[...]
        xo_ref[...] = x
    else:
        x = x_ref[...]
    u_scr[...] = (x * (1.0 + scale_ref[...]) + shift_ref[...]).astype(BF16)

    def project(c0, c1):
        return jnp.dot(u_scr[...], w_ref[:, c0:c1], preferred_element_type=F32)

    pool_width = widths[0]
    pos = t * rows + lax.broadcasted_iota(jnp.int32, (rows, 1), 0)
    pool_step = min(INPROJ_COLS, pool_width)
    for c in range(0, pool_width, pool_step):
        acc = project(c, c + pool_step)
        for j in range(pool_step // POOL_GROUP_DIM):
            lo = c + j * POOL_GROUP_DIM
            pm = jnp.dot(acc[:, j * POOL_GROUP_DIM:(j + 1) * POOL_GROUP_DIM].astype(BF16),
                         wgrp_ref[lo:lo + POOL_GROUP_DIM, :], preferred_element_type=F32)
            prev = jnp.where(first_tile, 0.0, tail_scr[:, lo:lo + POOL_GROUP_DIM])
            tail_scr[:, lo:lo + POOL_GROUP_DIM] = pm[rows - MAX_POOL_WINDOW:, :]
            mixed_scr[:, lo:lo + POOL_GROUP_DIM] = _pool_window(
                lo // POOL_GROUP_DIM, pm, prev, pos)
    col = pool_width
    for c in range(0, pool_width, pool_step):
        acc = project(col + c, col + c + pool_step)
        ypool_ref[:, c:c + pool_step] = (
            mixed_scr[:, c:c + pool_step] * pscale_ref[:, c:c + pool_step]
            * jax.nn.silu(acc)).astype(BF16)
    col += pool_width

    epilogues = (
        (q_ref, lambda a: a * (HEAD_DIM ** -0.5 * LOG2_E)),
        (k_ref, None),
        (v_ref, None),
        (agate_ref, jax.nn.silu),
        (g_ref, jax.nn.sigmoid),
    )
    for (o_ref, fn), width in zip(epilogues, widths[2:]):
        step = min(INPROJ_COLS, width)
        for c in range(0, width, step):
            acc = project(col + c, col + c + step)
            if fn is not None:
                acc = fn(acc)
            o_ref[:, c:c + step] = acc.astype(o_ref.dtype)
        col += width

    for src, dst in zip(slabs_in, slabs_out):
        dst[...] = src[...].astype(BF16)


def _inproj_call(x, scale, shift, w_in, wgrp, pscale, widths, seq, convert=(), norm=None,
                 hbm_weight_layer=None):
    m, d = x.shape
    n_in = w_in.shape[-1]
    steps = m // INPROJ_ROWS
    tiles_per_seq = seq // INPROJ_ROWS
    pool_width = widths[0]
    assert widths[1] == pool_width and pool_width % min(INPROJ_COLS, pool_width) == 0
    row_spec = lambda width: pl.BlockSpec((INPROJ_ROWS, width), lambda i: (i, 0))
    full = lambda a: pl.BlockSpec(a.shape, lambda i: (0,) * a.ndim,
                                  pipeline_mode=pl.Buffered(1))
    mod_spec = pl.BlockSpec((None, 1, d), lambda i: (i // tiles_per_seq, 0, 0))
    out_widths = (pool_width,) + tuple(widths[2:])
    out_shape = [jax.ShapeDtypeStruct((m, width), BF16) for width in out_widths]
    out_specs = [row_spec(width) for width in out_widths]
    vmem = (d * n_in * 2 + 2 * INPROJ_ROWS * d * 4 + 2 * INPROJ_ROWS * sum(out_widths) * 2
            + INPROJ_ROWS * d * 2 + INPROJ_ROWS * pool_width * 4 + TEMPORARIES_BYTES)
    slab_in_specs, slab_out_specs = [], []
    for a, layer in convert:
        _, r, c = a.shape
        slab = r // steps
        assert slab * steps == r and slab % BF16_SUBLANES == 0
        slab_in_specs.append(pl.BlockSpec((None, slab, c), lambda i, layer=layer: (layer, i, 0)))
        slab_out_specs.append(pl.BlockSpec((slab, c), lambda i: (i, 0)))
        out_shape.append(jax.ShapeDtypeStruct((r, c), BF16))
        vmem += 2 * slab * c * (4 + 2)
    norm_specs, norm_args = [], []
    if norm is not None:
        mean, rstd, gain, bias = norm
        norm_specs = [row_spec(LANES), row_spec(LANES), full(gain), full(bias)]
        norm_args = [mean, rstd, gain, bias]
        out_shape.insert(0, jax.ShapeDtypeStruct((m, d), F32))
        out_specs.insert(0, row_spec(d))
        vmem += 2 * INPROJ_ROWS * d * 4
    scratch = [pltpu.VMEM((INPROJ_ROWS, d), BF16),
               pltpu.VMEM((INPROJ_ROWS, pool_width), F32),
               pltpu.VMEM((MAX_POOL_WINDOW, pool_width), F32)]
    if hbm_weight_layer is None:
        w_spec = full(w_in)
    else:
        assert d % W_LOAD_ROWS == 0 and d // W_LOAD_ROWS >= W_LOAD_BUFFERS
        w_spec = pl.BlockSpec(memory_space=pl.ANY)
        scratch += [pltpu.VMEM((d, n_in), BF16),
                    pltpu.VMEM((W_LOAD_BUFFERS, W_LOAD_ROWS, n_in), F32),
                    pltpu.SemaphoreType.DMA((W_LOAD_BUFFERS,))]
        vmem += W_LOAD_BUFFERS * W_LOAD_ROWS * n_in * 4
    return pl.pallas_call(
        functools.partial(_inproj_kernel, widths=widths, n_convert=len(convert),
                          tiles_per_seq=tiles_per_seq, norm_input=norm is not None,
                          hbm_weight_layer=hbm_weight_layer),
        grid=(steps,),
        in_specs=[row_spec(d), mod_spec, mod_spec, w_spec, full(wgrp), full(pscale)]
        + norm_specs + slab_in_specs,
        out_specs=out_specs + slab_out_specs,
        out_shape=out_shape,
        scratch_shapes=scratch,
        compiler_params=pltpu.CompilerParams(dimension_semantics=("arbitrary",),
                                             vmem_limit_bytes=_vmem_limit(vmem)),
        name="inproj",
    )(x, scale, shift, w_in, wgrp, pscale, *norm_args, *[a for a, _ in convert])


def _attention(q_ref, k_ref, kprev_ref, v_ref, vprev_ref, sinks_ref, first_tile):
    rows = q_ref.shape[0]
    stack = Q_PER_KV * ATTN_BLOCK
    kext = jnp.concatenate([kprev_ref[...], k_ref[...]], axis=0)
    vext = jnp.concatenate([vprev_ref[...], v_ref[...]], axis=0)
    kj = lax.broadcasted_iota(jnp.int32, (2 * ATTN_BLOCK, stack), 0)
    qi = lax.broadcasted_iota(jnp.int32, (2 * ATTN_BLOCK, stack), 1) & (ATTN_BLOCK - 1)
    band = (kj > qi) & (kj <= qi + ATTN_BLOCK)
    bias_inner = jnp.where(band, 0.0, -jnp.inf).astype(F32)
    bias_first = jnp.where(band & ((kj >= ATTN_BLOCK) | jnp.logical_not(first_tile)),
                           0.0, -jnp.inf).astype(F32)
    blocks = []
    for qb in range(rows // ATTN_BLOCK):
        r0 = qb * ATTN_BLOCK
        bias = bias_first if qb == 0 else bias_inner
        pieces = []
        for h in range(N_KV_HEADS):
            qh = q_ref[r0:r0 + ATTN_BLOCK, h * Q_PER_KV * HEAD_DIM:(h + 1) * Q_PER_KV * HEAD_DIM]
            qs = jnp.concatenate(
                [qh[:, g * HEAD_DIM:(g + 1) * HEAD_DIM] for g in range(Q_PER_KV)], axis=0)
            kh = kext[r0:r0 + 2 * ATTN_BLOCK, h * HEAD_DIM:(h + 1) * HEAD_DIM]
            vh = vext[r0:r0 + 2 * ATTN_BLOCK, h * HEAD_DIM:(h + 1) * HEAD_DIM]
            st = lax.dot_general(kh, qs, (((1,), (1,)), ((), ())),
                                 preferred_element_type=F32) + bias
            sink = LOG2_E * jnp.concatenate(
                [jnp.full((1, ATTN_BLOCK), sinks_ref[h * Q_PER_KV + g], F32)
                 for g in range(Q_PER_KV)], axis=1)
            mx = jnp.maximum(jnp.max(st, axis=0, keepdims=True), sink)
            e = jnp.exp2(st - mx)
            denom = jnp.sum(e, axis=0, keepdims=True) + jnp.exp2(sink - mx)
            probs = (e * (1.0 / denom)).astype(BF16)
            o = lax.dot_general(probs, vh, (((0,), (0,)), ((), ())),
                                preferred_element_type=F32)
            pieces.extend(o[g * ATTN_BLOCK:(g + 1) * ATTN_BLOCK] for g in range(Q_PER_KV))
        blocks.append(jnp.concatenate(pieces, axis=1))
    return jnp.concatenate(blocks, axis=0)


def _mix_kernel(sinks_ref, ypool_ref, q_ref, k_ref, kprev_ref, v_ref, vprev_ref, agate_ref,
                g_ref, x_ref, gate_ref, wpu_ref, wau_ref, wout_ref, *rest, alpha, apply_norm):
    if apply_norm:
        lng_ref, lnb_ref, xo_ref, yattn_scr, merged_scr, cen_ref = rest
    else:
        cen_ref, mu_ref, rstd_ref, yattn_scr, merged_scr = rest
    rows, d = x_ref.shape
    first_tile = pl.program_id(1) == 0

    attn = _attention(q_ref, k_ref, kprev_ref, v_ref, vprev_ref, sinks_ref, first_tile)
    y_attn = (attn * agate_ref[...].astype(F32)).astype(BF16)

    yattn_scr[...] = y_attn
    for lo in range(0, d, MIX_OUT_COLS):
        hi = lo + MIX_OUT_COLS
        up_pool = jnp.dot(ypool_ref[...], wpu_ref[:, lo:hi], preferred_element_type=F32)
        up_attn = jnp.dot(yattn_scr[...], wau_ref[:, lo:hi], preferred_element_type=F32)
        merged_scr[:, lo:hi] = (g_ref[:, lo:hi].astype(F32) * up_pool
                                + g_ref[:, d + lo:d + hi].astype(F32) * up_attn).astype(BF16)

    lanes = LANES
    s1 = jnp.zeros((rows, lanes), F32)
    s2 = jnp.zeros((rows, lanes), F32)
    shift = None
    for lo in range(0, d, MIX_OUT_COLS):
        hi = lo + MIX_OUT_COLS
        out = jnp.dot(merged_scr[...], wout_ref[:, lo:hi], preferred_element_type=F32)
        y = alpha * x_ref[:, lo:hi] + gate_ref[:, lo:hi] * out
        if shift is None:
            shift = jnp.mean(y, axis=-1, keepdims=True)
        cen = y - shift
        cen_ref[:, lo:hi] = cen
        for j in range(0, MIX_OUT_COLS, lanes):
            piece = cen[:, j:j + lanes]
            s1 = s1 + piece
            s2 = s2 + piece * piece
    mu = jnp.sum(s1, axis=-1, keepdims=True) * (1.0 / d)
    var = jnp.sum(s2, axis=-1, keepdims=True) * (1.0 / d) - mu * mu
    rstd = lax.rsqrt(var + LN_EPS)
    if apply_norm:
        xo_ref[...] = (cen_ref[...] - mu) * rstd * lng_ref[...] + lnb_ref[...]
    else:
        mu_ref[...] = jnp.broadcast_to(mu, mu_ref.shape)
        rstd_ref[...] = jnp.broadcast_to(rstd, rstd_ref.shape)


def _mix_call(h_parts, x, gate, sinks, wpu, wau, wout, ln, batch, seq, alpha):
    ypool, q, k, v, agate, g = h_parts
    m, d = x.shape
    nt = seq // MIX_ROWS
    apply_norm = ln is not None

    row = lambda width: pl.BlockSpec((MIX_ROWS, width), lambda b, t: (b * nt + t, 0))

    def prev_spec(width):
        per_tile, per_seq = MIX_ROWS // ATTN_BLOCK, seq // ATTN_BLOCK
        return pl.BlockSpec(
            (ATTN_BLOCK, width),
            lambda b, t: (b * per_seq + jnp.maximum(t * per_tile - 1, 0), 0))

    full = lambda a: pl.BlockSpec(a.shape, lambda b, t: (0,) * a.ndim,
                                  pipeline_mode=pl.Buffered(1))
    mod_spec = pl.BlockSpec((None, 1, d), lambda b, t: (b, 0, 0))

    in_specs = [
        pl.BlockSpec(memory_space=pltpu.SMEM),
        row(ypool.shape[1]), row(q.shape[1]),
        row(k.shape[1]), prev_spec(k.shape[1]),
        row(v.shape[1]), prev_spec(v.shape[1]),
        row(agate.shape[1]), row(g.shape[1]), row(d), mod_spec,
        full(wpu), full(wau), full(wout),
    ]
    args = [sinks, ypool, q, k, k, v, v, agate, g, x, gate, wpu, wau, wout]
    scratch_shapes = [pltpu.VMEM((MIX_ROWS, q.shape[1]), BF16),
                      pltpu.VMEM((MIX_ROWS, d), BF16)]
    if apply_norm:
        in_specs += [full(ln[0]), full(ln[1])]
        args += list(ln)
        out_specs = row(d)
        out_shape = jax.ShapeDtypeStruct((m, d), F32)
        scratch_shapes.append(pltpu.VMEM((MIX_ROWS, d), F32))
    else:
        out_specs = [row(d), row(LANES), row(LANES)]
        out_shape = [jax.ShapeDtypeStruct((m, d), F32), jax.ShapeDtypeStruct((m, LANES), F32),
                     jax.ShapeDtypeStruct((m, LANES), F32)]

    h_cols = sum(a.shape[1] for a in h_parts)
    weights = 2 * (wpu.size + wau.size + wout.size)
    tiles = 2 * MIX_ROWS * (h_cols * 2 + d * 4 + d * 4)
    temps = 10 * MIX_ROWS * d * 4
    return pl.pallas_call(
        functools.partial(_mix_kernel, alpha=alpha, apply_norm=apply_norm),
        grid=(batch, nt),
        in_specs=in_specs,
        out_specs=out_specs,
        out_shape=out_shape,
        scratch_shapes=scratch_shapes,
        compiler_params=pltpu.CompilerParams(
            dimension_semantics=("arbitrary", "arbitrary"),
            vmem_limit_bytes=_vmem_limit(weights + tiles + temps),
        ),
        name="mixer",
    )(*args)


def kernel(x, c, w_ada, b_ada, w_in, w_pool_grp, pool_scale, sinks, w_pool_up, w_attn_up,
           w_out, ln_g, ln_b):
    batch, seq, d = x.shape
    depth = w_ada.shape[0]
    pool_width = w_pool_grp.shape[1] * w_pool_grp.shape[2]
    attn_width = w_attn_up.shape[1]
    kv_width = N_KV_HEADS * HEAD_DIM
    widths = (pool_width, pool_width, attn_width, kv_width, kv_width, attn_width, 2 * d)
    assert sum(widths) == w_in.shape[2]
    assert seq % MIX_ROWS == 0 and seq % INPROJ_ROWS == 0 and MIX_ROWS % ATTN_BLOCK == 0
    alpha = (2 * depth) ** 0.25

    mod = _ada_call(c, w_ada, b_ada)
    shift, scale, gate = [mod[:, :, i * d:(i + 1) * d].reshape(depth, batch, 1, d)
                          for i in range(3)]

    stacks = {"in": w_in, "grp": w_pool_grp.reshape(depth, pool_width, -1),
              "pool_up": w_pool_up, "attn_up": w_attn_up, "out": w_out}
    bf16_w = {("grp", 0): stacks["grp"][0].astype(BF16)}
    pending = [(name, l) for l in range(depth) for name in stacks
               if (name, l) not in bf16_w and (name, l) != ("in", 0)]

    ln = [(ln_g[l].reshape(1, d), ln_b[l].reshape(1, d)) for l in range(depth)]
    xf = x.reshape(batch * seq, d)
    pending_norm = None
    for l in range(depth):
        if pending_norm is None:
            outs = _inproj_call(xf, scale[l], shift[l], w_in, bf16_w["grp", l],
                                pool_scale[l].reshape(1, -1), widths, seq,
                                convert=[(stacks[name], j) for name, j in pending],
                                hbm_weight_layer=l)
            h_parts = outs[:6]
            bf16_w.update(zip(pending, outs[6:]))
        else:
            cen, mean, rstd = pending_norm
            xf, *h_parts = _inproj_call(cen, scale[l], shift[l], bf16_w["in", l],
                                        bf16_w["grp", l], pool_scale[l].reshape(1, -1), widths,
                                        seq, norm=(mean, rstd) + ln[l - 1])
        last = l == depth - 1
        res = _mix_call(h_parts, xf, gate[l], sinks[l], bf16_w["pool_up", l],
                        bf16_w["attn_up", l], bf16_w["out", l], ln[l] if last else None,
                        batch, seq, alpha)
        if last:
            xf = res
        else:
            pending_norm = res
    return xf.reshape(batch, seq, d)
```

```python
import functools

import jax
import jax.numpy as jnp
from jax import lax
from jax.experimental import pallas as pl
from jax.experimental.pallas import tpu as pltpu

F32 = jnp.float32
BF16 = jnp.bfloat16

POOL_WINDOWS = (2, 4, 8, 16)
POOL_GROUP_DIM = 256
MAX_POOL_WINDOW = 16
HEAD_DIM = 64
N_KV_HEADS = 4
Q_PER_KV = 4
ATTN_BLOCK = 128
LN_EPS = 1e-5
LOG2_E = 1.4426950408889634

MIB = 1024 * 1024
V7X_VMEM_LIMIT_CAP = 60000 * 1024
V7X_VMEM_LIMIT_DEFAULT = 32 * MIB
BF16_SUBLANES = 16
LANES = 128
TEMPORARIES_BYTES = 8 * MIB

ADA_COLS = 1536
INPROJ_ROWS = 256
INPROJ_COLS = 1024
W_LOAD_ROWS = BF16_SUBLANES
W_LOAD_BUFFERS = 5
MIX_ROWS = 512
MIX_OUT_COLS = 512


def _vmem_limit(estimate_bytes):
    return int(min(V7X_VMEM_LIMIT_CAP, max(V7X_VMEM_LIMIT_DEFAULT, estimate_bytes)))


def _ada_kernel(c_ref, w_ref, b_ref, o_ref):
    sc = jax.nn.silu(c_ref[...]).astype(BF16)
    acc = jnp.dot(sc, w_ref[...].astype(BF16), preferred_element_type=F32)
    o_ref[...] = acc + b_ref[...]


def _ada_call(c, w_ada, b_ada):
    depth, d, n = w_ada.shape
    b = c.shape[0]
    return pl.pallas_call(
        _ada_kernel,
        grid=(depth, n // ADA_COLS),
        in_specs=[
            pl.BlockSpec((b, d), lambda l, j: (0, 0)),
            pl.BlockSpec((None, d, ADA_COLS), lambda l, j: (l, 0, j)),
            pl.BlockSpec((None, 1, ADA_COLS), lambda l, j: (l, 0, j)),
        ],
        out_specs=pl.BlockSpec((None, b, ADA_COLS), lambda l, j: (l, 0, j)),
        out_shape=jax.ShapeDtypeStruct((depth, b, n), F32),
        compiler_params=pltpu.CompilerParams(
            dimension_semantics=("arbitrary", "arbitrary"),
            vmem_limit_bytes=_vmem_limit(2 * d * ADA_COLS * 4 + TEMPORARIES_BYTES),
        ),
        name="ada_mod",
    )(c, w_ada, b_ada.reshape(depth, 1, n))


def _pool_window(g, p, prev, pos):
    w = POOL_WINDOWS[g]
    s = jnp.concatenate([prev, p], axis=0)
    span = 1
    while span < w:
        s = s + pltpu.roll(s, span, 0)
        span *= 2
    win_sum = s[MAX_POOL_WINDOW:, :]
    count = jnp.minimum(pos + 1, w).astype(F32)
    return win_sum / count - p


def _load_weight(w_hbm, layer, w_scr, stage_scr, sems):
    n_buf, chunk, _ = stage_scr.shape
    n_chunks = w_scr.shape[0] // chunk

    def copy(c):
        slot = c % n_buf
        return pltpu.make_async_copy(w_hbm.at[layer, pl.ds(c * chunk, chunk), :],
                                     stage_scr.at[slot], sems.at[slot])

    for c in range(n_buf - 1):
        copy(c).start(priority=c % 2)

    def body(pair, carry):
        for parity in range(2):
            c = 2 * pair + parity
            @pl.when(c + n_buf - 1 < n_chunks)
            def _():
                copy(c + n_buf - 1).start(priority=(parity + n_buf - 1) % 2)
            copy(c).wait()
            w_scr[pl.ds(pl.multiple_of(c * chunk, chunk), chunk), :] = (
                stage_scr[c % n_buf].astype(BF16))
        return carry

    lax.fori_loop(0, n_chunks // 2, body, 0)


def _inproj_kernel(x_ref, scale_ref, shift_ref, w_ref, wgrp_ref, pscale_ref, *rest,
                   widths, n_convert, tiles_per_seq, norm_input, hbm_weight_layer):
    if norm_input:
        mu_ref, rstd_ref, lng_ref, lnb_ref = rest[:4]
        rest = rest[4:]
    slabs_in, rest = rest[:n_convert], rest[n_convert:]
    if norm_input:
        xo_ref, rest = rest[0], rest[1:]
    ypool_ref, q_ref, k_ref, v_ref, agate_ref, g_ref = rest[:6]
    slabs_out = rest[6:6 + n_convert]
    u_scr, mixed_scr, tail_scr, *load_scr = rest[6 + n_convert:]
    rows = x_ref.shape[0]
    i = pl.program_id(0)
    t = i % tiles_per_seq
    first_tile = t == 0

    @pl.when(i == 0)
    def _():
        tail_scr[...] = jnp.zeros(tail_scr.shape, tail_scr.dtype)
        if hbm_weight_layer is not None:
            _load_weight(w_ref, hbm_weight_layer, *load_scr)

    if hbm_weight_layer is not None:
        w_ref = load_scr[0]

    if norm_input:
        reps = x_ref.shape[1] // LANES
        mu = jnp.concatenate([mu_ref[...]] * reps, axis=1)
        rstd = jnp.concatenate([rstd_ref[...]] * reps, axis=1)
        x = (x_ref[...] - mu) * rstd * lng_ref[...] + lnb_ref[...]
        xo_ref[...] = x
    else:
        x = x_ref[...]
    u_scr[...] = (x * (1.0 + scale_ref[...]) + shift_ref[...]).astype(BF16)

    def project(c0, c1):
        return jnp.dot(u_scr[...], w_ref[:, c0:c1], preferred_element_type=F32)

    pool_width = widths[0]
    pos = t * rows + lax.broadcasted_iota(jnp.int32, (rows, 1), 0)
    pool_step = min(INPROJ_COLS, pool_width)
    for c in range(0, pool_width, pool_step):
        acc = project(c, c + pool_step)
        for j in range(pool_step // POOL_GROUP_DIM):
            lo = c + j * POOL_GROUP_DIM
            pm = jnp.dot(acc[:, j * POOL_GROUP_DIM:(j + 1) * POOL_GROUP_DIM].astype(BF16),
                         wgrp_ref[lo:lo + POOL_GROUP_DIM, :], preferred_element_type=F32)
            prev = jnp.where(first_tile, 0.0, tail_scr[:, lo:lo + POOL_GROUP_DIM])
            tail_scr[:, lo:lo + POOL_GROUP_DIM] = pm[rows - MAX_POOL_WINDOW:, :]
            mixed_scr[:, lo:lo + POOL_GROUP_DIM] = _pool_window(
                lo // POOL_GROUP_DIM, pm, prev, pos)
    col = pool_width
    for c in range(0, pool_width, pool_step):
        acc = project(col + c, col + c + pool_step)
        ypool_ref[:, c:c + pool_step] = (
            mixed_scr[:, c:c + pool_step] * pscale_ref[:, c:c + pool_step]
            * jax.nn.silu(acc)).astype(BF16)
    col += pool_width

    epilogues = (
        (q_ref, lambda a: a * (HEAD_DIM ** -0.5 * LOG2_E)),
        (k_ref, None),
        (v_ref, None),
        (agate_ref, jax.nn.silu),
        (g_ref, jax.nn.sigmoid),
    )
    for (o_ref, fn), width in zip(epilogues, widths[2:]):
        step = min(INPROJ_COLS, width)
        for c in range(0, width, step):
            acc = project(col + c, col + c + step)
            if fn is not None:
                acc = fn(acc)
            o_ref[:, c:c + step] = acc.astype(o_ref.dtype)
        col += width

    for src, dst in zip(slabs_in, slabs_out):
        dst[...] = src[...].astype(BF16)


def _inproj_call(x, scale, shift, w_in, wgrp, pscale, widths, seq, convert=(), norm=None,
                 hbm_weight_layer=None):
    m, d = x.shape
    n_in = w_in.shape[-1]
    steps = m // INPROJ_ROWS
    tiles_per_seq = seq // INPROJ_ROWS
    pool_width = widths[0]
    assert widths[1] == pool_width and pool_width % min(INPROJ_COLS, pool_width) == 0
    row_spec = lambda width: pl.BlockSpec((INPROJ_ROWS, width), lambda i: (i, 0))
    full = lambda a: pl.BlockSpec(a.shape, lambda i: (0,) * a.ndim,
                                  pipeline_mode=pl.Buffered(1))
    mod_spec = pl.BlockSpec((None, 1, d), lambda i: (i // tiles_per_seq, 0, 0))
    out_widths = (pool_width,) + tuple(widths[2:])
    out_shape = [jax.ShapeDtypeStruct((m, width), BF16) for width in out_widths]
    out_specs = [row_spec(width) for width in out_widths]
    vmem = (d * n_in * 2 + 2 * INPROJ_ROWS * d * 4 + 2 * INPROJ_ROWS * sum(out_widths) * 2
            + INPROJ_ROWS * d * 2 + INPROJ_ROWS * pool_width * 4 + TEMPORARIES_BYTES)
    slab_in_specs, slab_out_specs = [], []
    for a, layer in convert:
        _, r, c = a.shape
        slab = r // steps
        assert slab * steps == r and slab % BF16_SUBLANES == 0
        slab_in_specs.append(pl.BlockSpec((None, slab, c), lambda i, layer=layer: (layer, i, 0)))
        slab_out_specs.append(pl.BlockSpec((slab, c), lambda i: (i, 0)))
        out_shape.append(jax.ShapeDtypeStruct((r, c), BF16))
        vmem += 2 * slab * c * (4 + 2)
    norm_specs, norm_args = [], []
    if norm is not None:
        mean, rstd, gain, bias = norm
        norm_specs = [row_spec(LANES), row_spec(LANES), full(gain), full(bias)]
        norm_args = [mean, rstd, gain, bias]
        out_shape.insert(0, jax.ShapeDtypeStruct((m, d), F32))
        out_specs.insert(0, row_spec(d))
        vmem += 2 * INPROJ_ROWS * d * 4
    scratch = [pltpu.VMEM((INPROJ_ROWS, d), BF16),
               pltpu.VMEM((INPROJ_ROWS, pool_width), F32),
               pltpu.VMEM((MAX_POOL_WINDOW, pool_width), F32)]
    if hbm_weight_layer is None:
        w_spec = full(w_in)
    else:
        assert d % (2 * W_LOAD_ROWS) == 0 and d // W_LOAD_ROWS >= W_LOAD_BUFFERS
        w_spec = pl.BlockSpec(memory_space=pl.ANY)
        scratch += [pltpu.VMEM((d, n_in), BF16),
                    pltpu.VMEM((W_LOAD_BUFFERS, W_LOAD_ROWS, n_in), F32),
                    pltpu.SemaphoreType.DMA((W_LOAD_BUFFERS,))]
        vmem += W_LOAD_BUFFERS * W_LOAD_ROWS * n_in * 4
    return pl.pallas_call(
        functools.partial(_inproj_kernel, widths=widths, n_convert=len(convert),
                          tiles_per_seq=tiles_per_seq, norm_input=norm is not None,
                          hbm_weight_layer=hbm_weight_layer),
        grid=(steps,),
        in_specs=[row_spec(d), mod_spec, mod_spec, w_spec, full(wgrp), full(pscale)]
        + norm_specs + slab_in_specs,
        out_specs=out_specs + slab_out_specs,
        out_shape=out_shape,
        scratch_shapes=scratch,
        compiler_params=pltpu.CompilerParams(dimension_semantics=("arbitrary",),
                                             vmem_limit_bytes=_vmem_limit(vmem)),
        name="inproj",
    )(x, scale, shift, w_in, wgrp, pscale, *norm_args, *[a for a, _ in convert])


def _attention(q_ref, k_ref, kprev_ref, v_ref, vprev_ref, sinks_ref, first_tile):
    rows = q_ref.shape[0]
    stack = Q_PER_KV * ATTN_BLOCK
    kext = jnp.concatenate([kprev_ref[...], k_ref[...]], axis=0)
    vext = jnp.concatenate([vprev_ref[...], v_ref[...]], axis=0)
    kj = lax.broadcasted_iota(jnp.int32, (2 * ATTN_BLOCK, stack), 0)
    qi = lax.broadcasted_iota(jnp.int32, (2 * ATTN_BLOCK, stack), 1) & (ATTN_BLOCK - 1)
    band = (kj > qi) & (kj <= qi + ATTN_BLOCK)
    bias_inner = jnp.where(band, 0.0, -jnp.inf).astype(F32)
    bias_first = jnp.where(band & ((kj >= ATTN_BLOCK) | jnp.logical_not(first_tile)),
                           0.0, -jnp.inf).astype(F32)
    blocks = []
    for qb in range(rows // ATTN_BLOCK):
        r0 = qb * ATTN_BLOCK
        bias = bias_first if qb == 0 else bias_inner
        pieces = []
        for h in range(N_KV_HEADS):
            qh = q_ref[r0:r0 + ATTN_BLOCK, h * Q_PER_KV * HEAD_DIM:(h + 1) * Q_PER_KV * HEAD_DIM]
            qs = jnp.concatenate(
                [qh[:, g * HEAD_DIM:(g + 1) * HEAD_DIM] for g in range(Q_PER_KV)], axis=0)
            kh = kext[r0:r0 + 2 * ATTN_BLOCK, h * HEAD_DIM:(h + 1) * HEAD_DIM]
            vh = vext[r0:r0 + 2 * ATTN_BLOCK, h * HEAD_DIM:(h + 1) * HEAD_DIM]
            st = lax.dot_general(kh, qs, (((1,), (1,)), ((), ())),
                                 preferred_element_type=F32) + bias
            sink = LOG2_E * jnp.concatenate(
                [jnp.full((1, ATTN_BLOCK), sinks_ref[h * Q_PER_KV + g], F32)
                 for g in range(Q_PER_KV)], axis=1)
            mx = jnp.maximum(jnp.max(st, axis=0, keepdims=True), sink)
            e = jnp.exp2(st - mx)
            denom = jnp.sum(e, axis=0, keepdims=True) + jnp.exp2(sink - mx)
            probs = (e * (1.0 / denom)).astype(BF16)
            o = lax.dot_general(probs, vh, (((0,), (0,)), ((), ())),
                                preferred_element_type=F32)
            pieces.extend(o[g * ATTN_BLOCK:(g + 1) * ATTN_BLOCK] for g in range(Q_PER_KV))
        blocks.append(jnp.concatenate(pieces, axis=1))
    return jnp.concatenate(blocks, axis=0)


def _mix_kernel(sinks_ref, ypool_ref, q_ref, k_ref, kprev_ref, v_ref, vprev_ref, agate_ref,
                g_ref, x_ref, gate_ref, wpu_ref, wau_ref, wout_ref, *rest, alpha, apply_norm):
    if apply_norm:
        lng_ref, lnb_ref, xo_ref, yattn_scr, merged_scr, cen_ref = rest
    else:
        cen_ref, mu_ref, rstd_ref, yattn_scr, merged_scr = rest
    rows, d = x_ref.shape
    first_tile = pl.program_id(1) == 0

    attn = _attention(q_ref, k_ref, kprev_ref, v_ref, vprev_ref, sinks_ref, first_tile)
    y_attn = (attn * agate_ref[...].astype(F32)).astype(BF16)

    yattn_scr[...] = y_attn
    for lo in range(0, d, MIX_OUT_COLS):
        hi = lo + MIX_OUT_COLS
        up_pool = jnp.dot(ypool_ref[...], wpu_ref[:, lo:hi], preferred_element_type=F32)
        up_attn = jnp.dot(yattn_scr[...], wau_ref[:, lo:hi], preferred_element_type=F32)
        merged_scr[:, lo:hi] = (g_ref[:, lo:hi].astype(F32) * up_pool
                                + g_ref[:, d + lo:d + hi].astype(F32) * up_attn).astype(BF16)

    lanes = LANES
    s1 = jnp.zeros((rows, lanes), F32)
    s2 = jnp.zeros((rows, lanes), F32)
    shift = None
    for lo in range(0, d, MIX_OUT_COLS):
        hi = lo + MIX_OUT_COLS
        out = jnp.dot(merged_scr[...], wout_ref[:, lo:hi], preferred_element_type=F32)
        y = alpha * x_ref[:, lo:hi] + gate_ref[:, lo:hi] * out
        if shift is None:
            shift = jnp.mean(y, axis=-1, keepdims=True)
        cen = y - shift
        cen_ref[:, lo:hi] = cen
        for j in range(0, MIX_OUT_COLS, lanes):
            piece = cen[:, j:j + lanes]
            s1 = s1 + piece
            s2 = s2 + piece * piece
    mu = jnp.sum(s1, axis=-1, keepdims=True) * (1.0 / d)
    var = jnp.sum(s2, axis=-1, keepdims=True) * (1.0 / d) - mu * mu
    rstd = lax.rsqrt(var + LN_EPS)
    if apply_norm:
        xo_ref[...] = (cen_ref[...] - mu) * rstd * lng_ref[...] + lnb_ref[...]
    else:
        mu_ref[...] = jnp.broadcast_to(mu, mu_ref.shape)
        rstd_ref[...] = jnp.broadcast_to(rstd, rstd_ref.shape)


def _mix_call(h_parts, x, gate, sinks, wpu, wau, wout, ln, batch, seq, alpha):
    ypool, q, k, v, agate, g = h_parts
    m, d = x.shape
    nt = seq // MIX_ROWS
    apply_norm = ln is not None

    row = lambda width: pl.BlockSpec((MIX_ROWS, width), lambda b, t: (b * nt + t, 0))

    def prev_spec(width):
        per_tile, per_seq = MIX_ROWS // ATTN_BLOCK, seq // ATTN_BLOCK
        return pl.BlockSpec(
            (ATTN_BLOCK, width),
            lambda b, t: (b * per_seq + jnp.maximum(t * per_tile - 1, 0), 0))

    full = lambda a: pl.BlockSpec(a.shape, lambda b, t: (0,) * a.ndim,
                                  pipeline_mode=pl.Buffered(1))
    mod_spec = pl.BlockSpec((None, 1, d), lambda b, t: (b, 0, 0))

    in_specs = [
        pl.BlockSpec(memory_space=pltpu.SMEM),
        row(ypool.shape[1]), row(q.shape[1]),
        row(k.shape[1]), prev_spec(k.shape[1]),
        row(v.shape[1]), prev_spec(v.shape[1]),
        row(agate.shape[1]), row(g.shape[1]), row(d), mod_spec,
        full(wpu), full(wau), full(wout),
    ]
    args = [sinks, ypool, q, k, k, v, v, agate, g, x, gate, wpu, wau, wout]
    scratch_shapes = [pltpu.VMEM((MIX_ROWS, q.shape[1]), BF16),
                      pltpu.VMEM((MIX_ROWS, d), BF16)]
    if apply_norm:
        in_specs += [full(ln[0]), full(ln[1])]
        args += list(ln)
        out_specs = row(d)
        out_shape = jax.ShapeDtypeStruct((m, d), F32)
        scratch_shapes.append(pltpu.VMEM((MIX_ROWS, d), F32))
    else:
        out_specs = [row(d), row(LANES), row(LANES)]
        out_shape = [jax.ShapeDtypeStruct((m, d), F32), jax.ShapeDtypeStruct((m, LANES), F32),
                     jax.ShapeDtypeStruct((m, LANES), F32)]

    h_cols = sum(a.shape[1] for a in h_parts)
    weights = 2 * (wpu.size + wau.size + wout.size)
    tiles = 2 * MIX_ROWS * (h_cols * 2 + d * 4 + d * 4)
    temps = 10 * MIX_ROWS * d * 4
    return pl.pallas_call(
        functools.partial(_mix_kernel, alpha=alpha, apply_norm=apply_norm),
        grid=(batch, nt),
        in_specs=in_specs,
        out_specs=out_specs,
        out_shape=out_shape,
        scratch_shapes=scratch_shapes,
        compiler_params=pltpu.CompilerParams(
            dimension_semantics=("arbitrary", "arbitrary"),
            vmem_limit_bytes=_vmem_limit(weights + tiles + temps),
        ),
        name="mixer",
    )(*args)


def kernel(x, c, w_ada, b_ada, w_in, w_pool_grp, pool_scale, sinks, w_pool_up, w_attn_up,
           w_out, ln_g, ln_b):
    batch, seq, d = x.shape
    depth = w_ada.shape[0]
    pool_width = w_pool_grp.shape[1] * w_pool_grp.shape[2]
    attn_width = w_attn_up.shape[1]
    kv_width = N_KV_HEADS * HEAD_DIM
    widths = (pool_width, pool_width, attn_width, kv_width, kv_width, attn_width, 2 * d)
    assert sum(widths) == w_in.shape[2]
    assert seq % MIX_ROWS == 0 and seq % INPROJ_ROWS == 0 and MIX_ROWS % ATTN_BLOCK == 0
    alpha = (2 * depth) ** 0.25

    mod = _ada_call(c, w_ada, b_ada)
    shift, scale, gate = [mod[:, :, i * d:(i + 1) * d].reshape(depth, batch, 1, d)
                          for i in range(3)]

    stacks = {"in": w_in, "grp": w_pool_grp.reshape(depth, pool_width, -1),
              "pool_up": w_pool_up, "attn_up": w_attn_up, "out": w_out}
    bf16_w = {("grp", 0): stacks["grp"][0].astype(BF16)}
    pending = [(name, l) for l in range(depth) for name in stacks
               if (name, l) not in bf16_w and (name, l) != ("in", 0)]

    ln = [(ln_g[l].reshape(1, d), ln_b[l].reshape(1, d)) for l in range(depth)]
    xf = x.reshape(batch * seq, d)
    pending_norm = None
    for l in range(depth):
        if pending_norm is None:
            outs = _inproj_call(xf, scale[l], shift[l], w_in, bf16_w["grp", l],
                                pool_scale[l].reshape(1, -1), widths, seq,
                                convert=[(stacks[name], j) for name, j in pending],
                                hbm_weight_layer=l)
            h_parts = outs[:6]
            bf16_w.update(zip(pending, outs[6:]))
        else:
            cen, mean, rstd = pending_norm
            xf, *h_parts = _inproj_call(cen, scale[l], shift[l], bf16_w["in", l],
                                        bf16_w["grp", l], pool_scale[l].reshape(1, -1), widths,
                                        seq, norm=(mean, rstd) + ln[l - 1])
        last = l == depth - 1
        res = _mix_call(h_parts, xf, gate[l], sinks[l], bf16_w["pool_up", l],
                        bf16_w["attn_up", l], bf16_w["out", l], ln[l] if last else None,
                        batch, seq, alpha)
        if last:
            xf = res
        else:
            pending_norm = res
    return xf.reshape(batch, seq, d)
```
